```python
import math
import jax, jax.numpy as jnp
from jax import lax
import numpy as np

D_MODEL = 1024
BATCH = 8
SEQ = 8192
DEPTH = 2

N_MIXERS = 2
HGRN_EXPAND = 128
HGRN_HEADS = D_MODEL // HGRN_EXPAND
HGRN_CHUNK = 64
DIFF_HEADS = 8
DIFF_HEAD_DIM = D_MODEL // DIFF_HEADS // 2
Q_BLOCK = 128
FFN_DIM = 2816
N_EXPERTS = 8
TOP_K = 2
EXPERT_DIM = 3584
MOE_SEQ_BLOCK = 128
NORM_EPS = 1e-6

kernel_name = "hgrn2_diffattn_moe_hybrid"

F32 = jnp.float32


def rms_norm(x, g):
    xf = x.astype(F32)
    y = xf * lax.rsqrt(jnp.mean(xf * xf, axis=-1, keepdims=True) + NORM_EPS)
    return (y * g.astype(F32)).astype(x.dtype)


def diff_lambda_init(layer_idx):
    return 0.8 - 0.6 * math.exp(-0.3 * layer_idx)


def alibi_slopes(n_heads):
    return 2.0 ** (-8.0 * jnp.arange(1, n_heads + 1, dtype=F32) / n_heads)


def hgrn2_mixer(h, w_in, out_norm, w_out, lower_bound):
    B, T, _ = h.shape
    H, dh, C = HGRN_HEADS, HGRN_EXPAND, HGRN_CHUNK
    nC = T // C
    q, f, v, g = jnp.split(h @ w_in, 4, axis=-1)
    q = jax.nn.silu(q.astype(F32))
    lb = lower_bound.astype(F32)
    forget = lb + (1.0 - lb) * jax.nn.sigmoid(f.astype(F32))
    k = 1.0 - forget
    log_f = jnp.log(forget)

    def chunks(a):
        return a.astype(F32).reshape(B, nC, C, H, dh).transpose(1, 0, 3, 2, 4)

    causal = jnp.tril(jnp.ones((C, C), dtype=bool))[None, None, :, :, None]

    def step(S, inp):
        qc, kc, vc, lc = inp
        b = jnp.cumsum(lc, axis=2)
        rel = b[:, :, :, None, :] - b[:, :, None, :, :]
        decay = jnp.exp(jnp.where(causal, rel, -jnp.inf))
        scores = jnp.einsum('bhtk,bhsk,bhtsk->bhts', qc, kc, decay)
        o = (jnp.einsum('bhts,bhsv->bhtv', scores, vc)
             + jnp.einsum('bhtk,bhkv->bhtv', qc * jnp.exp(b), S))
        b_end = b[:, :, -1:, :]
        S = (jnp.exp(b_end[:, :, 0, :])[..., None] * S
             + jnp.einsum('bhsk,bhsv->bhkv', kc * jnp.exp(b_end - b), vc))
        return S, o

    S0 = jnp.zeros((B, H, dh, dh), F32)
    _, o = lax.scan(step, S0, (chunks(q), chunks(k), chunks(v), chunks(log_f)))
    o = o.transpose(1, 0, 3, 2, 4).reshape(B, T, H, dh)
    o = rms_norm(o, out_norm.reshape(H, dh)) * jax.nn.silu(g.astype(F32)).reshape(B, T, H, dh)
    return o.reshape(B, T, D_MODEL).astype(h.dtype) @ w_out


def diff_attention_mixer(h, w_in, q_norm, k_norm, lq1, lk1, lq2, lk2, sub_norm, w_out, lambda_init):
    B, T, _ = h.shape
    H, dh, QB = DIFF_HEADS, DIFF_HEAD_DIM, Q_BLOCK
    nQ = T // QB
    q, k, v = jnp.split(h @ w_in, 3, axis=-1)
    q = rms_norm(q.reshape(B, T, H, 2, dh), q_norm)
    k = rms_norm(k.reshape(B, T, H, 2, dh), k_norm)
    v = v.reshape(B, T, H, 2 * dh)
    lam = (jnp.exp(jnp.sum(lq1.astype(F32) * lk1.astype(F32)))
           - jnp.exp(jnp.sum(lq2.astype(F32) * lk2.astype(F32))) + lambda_init)
    k_t = k.transpose(0, 2, 3, 1, 4).astype(F32)
    v_t = v.transpose(0, 2, 1, 3).astype(F32)
    q_blocks = q.reshape(B, nQ, QB, H, 2, dh).transpose(1, 0, 3, 4, 2, 5).astype(F32)
    slopes = alibi_slopes(H)[None, :, None, None, None]
    key_pos = jnp.arange(T)
    scale = dh ** -0.5

    def attend(args):
        qb, blk = args
        dist = (blk * QB + jnp.arange(QB))[:, None] - key_pos[None, :]
        logits = (jnp.einsum('bhcqd,bhckd->bhcqk', qb, k_t) * scale
                  - slopes * dist.astype(F32))
        logits = jnp.where(dist >= 0, logits, -jnp.inf)
        p = jax.nn.softmax(logits, axis=-1)
        w = p[:, :, 0] - lam * p[:, :, 1]
        return jnp.einsum('bhqk,bhkv->bhqv', w, v_t)

    o = lax.map(attend, (q_blocks, jnp.arange(nQ)))
    o = o.transpose(1, 0, 3, 2, 4).reshape(B, T, H, 2 * dh)
    o = rms_norm(o, sub_norm) * (1.0 - lambda_init)
    return o.reshape(B, T, D_MODEL).astype(h.dtype) @ w_out


def swiglu(h, w_gate_up, w_down):
    a, b = jnp.split(h @ w_gate_up, 2, axis=-1)
    return (jax.nn.silu(a) * b) @ w_down


def moe_swiglu(h, router, w_gate_up, w_down):
    B, T, Dm = h.shape
    nb = T // MOE_SEQ_BLOCK
    logits = (h @ router).astype(F32)
    top_val, top_idx = lax.top_k(logits, TOP_K)
    top_w = jax.nn.softmax(top_val, axis=-1)
    gates = jnp.sum(jax.nn.one_hot(top_idx, N_EXPERTS, dtype=F32) * top_w[..., None], axis=-2)

    def block(args):
        xb, gb = args
        a, b = jnp.split(jnp.einsum('bnd,edf->ebnf', xb, w_gate_up), 2, axis=-1)
        y = jnp.einsum('ebnf,efd->ebnd', jax.nn.silu(a) * b, w_down)
        return jnp.einsum('ebnd,bne->bnd', y, gb.astype(y.dtype))

    xs = h.reshape(B, nb, MOE_SEQ_BLOCK, Dm).transpose(1, 0, 2, 3)
    gs = gates.reshape(B, nb, MOE_SEQ_BLOCK, N_EXPERTS).transpose(1, 0, 2, 3)
    y = lax.map(block, (xs, gs))
    return y.transpose(1, 0, 2, 3).reshape(B, T, Dm)


def setup_inputs(seed: int = 0) -> dict:
    key = jax.random.key(seed)
    ks = jax.random.split(key, 24)
    D = D_MODEL
    nrm = lambda k, shape, s: jax.random.normal(k, shape, F32) * s
    gain = lambda k, n: 1.0 + 0.02 * jax.random.normal(k, (n,), F32)
    return {
        "x": jax.random.normal(ks[0], (BATCH, SEQ, D), F32),
        "lower_bounds": nrm(ks[1], (DEPTH + 1, D), 0.1),
        "l0_mix_norm": gain(ks[2], D),
        "l0_hgrn_w_in": nrm(ks[3], (D, 4 * D), D ** -0.5),
        "l0_hgrn_out_norm": gain(ks[4], D),
        "l0_hgrn_w_out": nrm(ks[5], (D, D), D ** -0.5),
        "l0_ffn_norm": gain(ks[6], D),
        "l0_ffn_w_gate_up": nrm(ks[7], (D, 2 * FFN_DIM), D ** -0.5),
        "l0_ffn_w_down": nrm(ks[8], (FFN_DIM, D), FFN_DIM ** -0.5),
        "l1_mix_norm": gain(ks[9], D),
        "l1_diff_w_in": nrm(ks[10], (D, 3 * D), D ** -0.5),
        "l1_q_norm": gain(ks[11], DIFF_HEAD_DIM),
        "l1_k_norm": gain(ks[12], DIFF_HEAD_DIM),
        "l1_lambda_q1": nrm(ks[13], (DIFF_HEAD_DIM,), 0.1),
        "l1_lambda_k1": nrm(ks[14], (DIFF_HEAD_DIM,), 0.1),
        "l1_lambda_q2": nrm(ks[15], (DIFF_HEAD_DIM,), 0.1),
        "l1_lambda_k2": nrm(ks[16], (DIFF_HEAD_DIM,), 0.1),
        "l1_diff_sub_norm": gain(ks[17], 2 * DIFF_HEAD_DIM),
        "l1_diff_w_out": nrm(ks[18], (D, D), D ** -0.5),
        "l1_ffn_norm": gain(ks[19], D),
        "l1_router": nrm(ks[20], (D, N_EXPERTS), D ** -0.5),
        "l1_moe_w_gate_up": nrm(ks[21], (N_EXPERTS, D, 2 * EXPERT_DIM), D ** -0.5),
        "l1_moe_w_down": nrm(ks[22], (N_EXPERTS, EXPERT_DIM, D), EXPERT_DIM ** -0.5),
    }


def reference(x, lower_bounds, l0_mix_norm, l0_hgrn_w_in, l0_hgrn_out_norm, l0_hgrn_w_out,
              l0_ffn_norm, l0_ffn_w_gate_up, l0_ffn_w_down, l1_mix_norm, l1_diff_w_in,
              l1_q_norm, l1_k_norm, l1_lambda_q1, l1_lambda_k1, l1_lambda_q2, l1_lambda_k2,
              l1_diff_sub_norm, l1_diff_w_out, l1_ffn_norm, l1_router, l1_moe_w_gate_up,
              l1_moe_w_down):
    lb_all = jnp.cumsum(jax.nn.softmax(lower_bounds.astype(F32), axis=0), axis=0)
    layers = [
        dict(mix_norm=l0_mix_norm, ffn_norm=l0_ffn_norm,
             mixer=(l0_hgrn_w_in, l0_hgrn_out_norm, l0_hgrn_w_out),
             ffn=(l0_ffn_w_gate_up, l0_ffn_w_down)),
        dict(mix_norm=l1_mix_norm, ffn_norm=l1_ffn_norm,
             mixer=(l1_diff_w_in, l1_q_norm, l1_k_norm, l1_lambda_q1, l1_lambda_k1,
                    l1_lambda_q2, l1_lambda_k2, l1_diff_sub_norm, l1_diff_w_out),
             ffn=(l1_router, l1_moe_w_gate_up, l1_moe_w_down)),
    ]
    h = x
    for i in range(DEPTH):
        p = layers[i]
        hn = rms_norm(h, p["mix_norm"])
        if i % N_MIXERS == 0:
            h = h + hgrn2_mixer(hn, *p["mixer"], lb_all[i])
        else:
            h = h + diff_attention_mixer(hn, *p["mixer"], diff_lambda_init(i))
        hn = rms_norm(h, p["ffn_norm"])
        if i % 2 == 0:
            h = h + swiglu(hn, *p["ffn"])
        else:
            h = h + moe_swiglu(hn, *p["ffn"])
    return h
```

```python
import functools
import math

import jax
import jax.numpy as jnp
from jax import lax
from jax.experimental import pallas as pl
from jax.experimental.pallas import tpu as pltpu

F32 = jnp.float32
BF16 = jnp.bfloat16
I32 = jnp.int32

NORM_EPS = 1e-6
LANE = 128
HGRN_CHUNK = 64
DIFF_HEAD_DIM = 64
TOP_K = 2
LOG2E = math.log2(math.e)
EXP_CLAMP = 60.0
VMEM_LIMIT = 56 * 1024 * 1024

TILES = dict(
    tm_in=512,
    tb_rec=512,
    tm_proj=512,
    tm_ffn=512,
    tq=512,
    tm_router=512,
    ts=512,
    tc=256,
    tmx=1024,
    fbx=512,
)


def _cparams(sem):
    return pltpu.CompilerParams(dimension_semantics=sem, vmem_limit_bytes=VMEM_LIMIT)


def _sigmoid(x):
    return 1.0 / (1.0 + jnp.exp(-x))


def _silu(x):
    return x * _sigmoid(x)


def _rms_rows(x, gain):
    ms = jnp.mean(x * x, axis=-1, keepdims=True)
    return x * lax.rsqrt(ms + NORM_EPS) * gain


def _dot(a, b):
    return jnp.dot(a, b, preferred_element_type=F32)


def _dot_nt(a, b):
    return lax.dot_general(a, b, (((1,), (1,)), ((), ())), preferred_element_type=F32)


def _dot_tn(a, b):
    return lax.dot_general(a, b, (((0,), (0,)), ((), ())), preferred_element_type=F32)


def _hgrn_in_kernel(x_ref, g_ref, lb_ref, w_ref, q_ref, k_ref, lf_ref, v_ref, gs_ref, xn_ref):
    j = pl.program_id(1)

    @pl.when(j == 0)
    def _():
        xn_ref[...] = _rms_rows(x_ref[...], g_ref[...]).astype(BF16)

    y = _dot(xn_ref[...], w_ref[...])

    @pl.when(j == 0)
    def _():
        q_ref[...] = _silu(y).astype(BF16)

    @pl.when(j == 1)
    def _():
        lb = lb_ref[...]
        forget = lb + (1.0 - lb) * _sigmoid(y)
        k_ref[...] = (1.0 - forget).astype(BF16)
        lf_ref[...] = jnp.log(forget)

    @pl.when(j == 2)
    def _():
        v_ref[...] = y.astype(BF16)

    @pl.when(j == 3)
    def _():
        gs_ref[...] = _silu(y).astype(BF16)


def _hgrn_in(x, gain, lb, w_bf16):
    n, d = x.shape
    tm = min(TILES["tm_in"], n)
    row = pl.BlockSpec((tm, d), lambda i, j: (i, 0))
    vec = pl.BlockSpec((1, d), lambda i, j: (0, 0))
    return pl.pallas_call(
        _hgrn_in_kernel,
        grid=(n // tm, 4),
        in_specs=[row, vec, vec, pl.BlockSpec((d, d), lambda i, j: (0, j))],
        out_specs=[row, row, row, row, row],
        out_shape=[
            jax.ShapeDtypeStruct((n, d), BF16),
            jax.ShapeDtypeStruct((n, d), BF16),
            jax.ShapeDtypeStruct((n, d), F32),
            jax.ShapeDtypeStruct((n, d), BF16),
            jax.ShapeDtypeStruct((n, d), BF16),
        ],
        scratch_shapes=[pltpu.VMEM((tm, d), BF16)],
        compiler_params=_cparams(("arbitrary", "arbitrary")),
        name="hgrn_in",
    )(x, gain, lb, w_bf16)


def _hgrn_rec_kernel(q_ref, k_ref, lf_ref, v_ref, gs_ref, on_ref, o_ref, st_ref, *, chunk, heads):
    @pl.when(pl.program_id(1) == 0)
    def _():
        st_ref[...] = jnp.zeros_like(st_ref)

    c = chunk
    n_chunks = q_ref.shape[0] // c
    row = lax.broadcasted_iota(I32, (c, c), 0)
    col = lax.broadcasted_iota(I32, (c, c), 1)
    causal = col <= row
    tri = causal.astype(BF16)
    mid = c // 2 - 1

    def body(ci, carry):
        r0 = pl.multiple_of(ci * c, c)
        rows = pl.ds(r0, c)
        lf = lf_ref[rows, :]
        hi = lf.astype(BF16)
        lo = (lf - hi.astype(F32)).astype(BF16)
        b = _dot(tri, hi) + _dot(tri, lo)
        bmid = b[mid:mid + 1, :]
        bend = b[c - 1:c, :]
        e_q = jnp.exp(jnp.minimum(b - bmid, EXP_CLAMP))
        e_k = jnp.exp(jnp.minimum(bmid - b, EXP_CLAMP))
        qt = (q_ref[rows, :].astype(F32) * e_q).astype(BF16)
        kt = (k_ref[rows, :].astype(F32) * e_k).astype(BF16)
        v = v_ref[rows, :]
        gs = gs_ref[rows, :].astype(F32)
        e_mid = jnp.exp(bmid)
        e_end = jnp.exp(bend)
        e_end_mid = jnp.exp(bend - bmid)
        for h in range(heads):
            sl = slice(h * LANE, (h + 1) * LANE)
            qh, kh, vh = qt[:, sl], kt[:, sl], v[:, sl]
            a = jnp.where(causal, _dot_nt(qh, kh), 0.0).astype(BF16)
            st = st_ref[h]
            o = _dot(a, vh) + _dot_nt(qh, (st * e_mid[:, sl]).astype(BF16))
            st_ref[h] = st * e_end[:, sl] + _dot_tn(vh, kh) * e_end_mid[:, sl]
            og = _rms_rows(o, on_ref[:, sl]) * gs[:, sl]
            o_ref[rows, sl] = og.astype(BF16)
        return carry

    lax.fori_loop(0, n_chunks, body, 0)


def _hgrn_rec(q, k, lf, v, gs, out_norm, batch, seq):
    n, d = q.shape
    heads = d // LANE
    tb = min(TILES["tb_rec"], seq)
    nt = seq // tb
    blk = pl.BlockSpec((tb, d), lambda b, t: (b * nt + t, 0))
    return pl.pallas_call(
        functools.partial(_hgrn_rec_kernel, chunk=HGRN_CHUNK, heads=heads),
        grid=(batch, nt),
        in_specs=[blk, blk, blk, blk, blk, pl.BlockSpec((1, d), lambda b, t: (0, 0))],
        out_specs=blk,
        out_shape=jax.ShapeDtypeStruct((n, d), BF16),
        scratch_shapes=[pltpu.VMEM((heads, LANE, LANE), F32)],
        compiler_params=_cparams(("arbitrary", "arbitrary")),
        name="hgrn_rec",
    )(q, k, lf, v, gs, out_norm)


def _proj_res_kernel(a_ref, w_ref, r_ref, o_ref):
    o_ref[...] = r_ref[...] + _dot(a_ref[...], w_ref[...])


def _proj_res(a, w_bf16, res):
    n, d = res.shape
    tm = min(TILES["tm_proj"], n)
    row = pl.BlockSpec((tm, d), lambda i: (i, 0))
    return pl.pallas_call(
        _proj_res_kernel,
        grid=(n // tm,),
        in_specs=[row, pl.BlockSpec((d, d), lambda i: (0, 0)), row],
        out_specs=row,
        out_shape=jax.ShapeDtypeStruct((n, d), F32),
        compiler_params=_cparams(("arbitrary",)),
        name="proj_res",
    )(a, w_bf16, res)


def _ffn_kernel(h_ref, g_ref, wg_ref, wu_ref, wd_ref, o_ref, xn_ref, acc_ref):
    j = pl.program_id(1)

    @pl.when(j == 0)
    def _():
        xn_ref[...] = _rms_rows(h_ref[...], g_ref[...]).astype(BF16)
        acc_ref[...] = jnp.zeros_like(acc_ref)

    xn = xn_ref[...]
    act = (_silu(_dot(xn, wg_ref[...])) * _dot(xn, wu_ref[...])).astype(BF16)
    acc_ref[...] += _dot(act, wd_ref[...])

    @pl.when(j == pl.num_programs(1) - 1)
    def _():
        o_ref[...] = h_ref[...] + acc_ref[...]


def _hidden_block(f, want):
    best = None
    for fb in range(LANE, f + 1, LANE):
        if f % fb == 0 and fb <= want:
            best = fb
    assert best is not None, f
    return best


def _ffn(h, gain, w_gate_up_bf16, w_down_bf16):
    n, d = h.shape
    f = w_down_bf16.shape[0]
    tm = min(TILES["tm_ffn"], n)
    fb = _hidden_block(f, 1408)
    nf = f // fb
    row = pl.BlockSpec((tm, d), lambda i, j: (i, 0))
    return pl.pallas_call(
        _ffn_kernel,
        grid=(n // tm, nf),
        in_specs=[
            row,
            pl.BlockSpec((1, d), lambda i, j: (0, 0)),
            pl.BlockSpec((d, fb), lambda i, j: (0, j)),
            pl.BlockSpec((d, fb), lambda i, j: (0, j + nf)),
            pl.BlockSpec((fb, d), lambda i, j: (j, 0)),
        ],
        out_specs=row,
        out_shape=jax.ShapeDtypeStruct((n, d), F32),
        scratch_shapes=[pltpu.VMEM((tm, d), BF16), pltpu.VMEM((tm, d), F32)],
        compiler_params=_cparams(("arbitrary", "arbitrary")),
        name="ffn",
    )(h, gain, w_gate_up_bf16, w_gate_up_bf16, w_down_bf16)


def _group_rms(y, gmat, gain):
    d = y.shape[1]
    sq = (y * y).astype(BF16)
    ms = jnp.concatenate(
        [_dot(sq[:, s * LANE:(s + 1) * LANE], gmat) for s in range(d // LANE)], axis=1)
    return y * lax.rsqrt(ms + NORM_EPS) * gain


def _diff_in_kernel(x_ref, g_ref, w_ref, gm_ref, qn_ref, kn_ref, q_ref, k_ref, v_ref, xn_ref):
    j = pl.program_id(1)

    @pl.when(j == 0)
    def _():
        xn_ref[...] = _rms_rows(x_ref[...], g_ref[...]).astype(BF16)

    y = _dot(xn_ref[...], w_ref[...])

    @pl.when(j == 0)
    def _():
        q_ref[...] = (_group_rms(y, gm_ref[...], qn_ref[...]) * (DIFF_HEAD_DIM ** -0.5 * LOG2E)).astype(BF16)

    @pl.when(j == 1)
    def _():
        k_ref[...] = _group_rms(y, gm_ref[...], kn_ref[...]).astype(BF16)

    @pl.when(j == 2)
    def _():
        v_ref[...] = y.astype(BF16)


def _diff_in(x, gain, w_bf16, q_norm_row, k_norm_row):
    n, d = x.shape
    tm = min(TILES["tm_in"], n)
    lane_group = jnp.arange(LANE) // DIFF_HEAD_DIM
    gmat = ((lane_group[:, None] == lane_group[None, :]).astype(F32) / DIFF_HEAD_DIM).astype(BF16)
    row = pl.BlockSpec((tm, d), lambda i, j: (i, 0))
    vec = pl.BlockSpec((1, d), lambda i, j: (0, 0))
    return pl.pallas_call(
        _diff_in_kernel,
        grid=(n // tm, 3),
        in_specs=[row, vec, pl.BlockSpec((d, d), lambda i, j: (0, j)),
                  pl.BlockSpec((LANE, LANE), lambda i, j: (0, 0)), vec, vec],
        out_specs=[row, row, row],
        out_shape=[jax.ShapeDtypeStruct((n, d), BF16)] * 3,
        scratch_shapes=[pltpu.VMEM((tm, d), BF16)],
        compiler_params=_cparams(("arbitrary", "arbitrary")),
        name="diff_in",
    )(x, gain, w_bf16, gmat, q_norm_row, k_norm_row)


def _attn_kernel(slope_ref, q_ref, k_ref, v_ref, lam_ref, sn_ref, o_ref, m_ref, l_ref, acc_ref,
                 *, tq, lam_init):
    h = pl.program_id(1)
    qb = pl.program_id(2)
    slope = slope_ref[h] * LOG2E

    q = q_ref[...]
    lane = lax.broadcasted_iota(I32, q.shape, 1)
    zero = jnp.zeros_like(q)
    qs = jnp.concatenate([jnp.where(lane < DIFF_HEAD_DIM, q, zero),
                          jnp.where(lane >= DIFF_HEAD_DIM, q, zero)], axis=0)

    m_ref[...] = jnp.full_like(m_ref, -jnp.inf)
    l_ref[...] = jnp.zeros_like(l_ref)
    acc_ref[...] = jnp.zeros_like(acc_ref)

    colf = lax.broadcasted_iota(I32, (1, tq), 1).astype(F32)

    def step(kb, masked):
        k0 = pl.multiple_of(kb * tq, tq)
        kt = k_ref[pl.ds(k0, tq), :]
        vt = v_ref[pl.ds(k0, tq), :]
        u = _dot_nt(qs, kt) + slope * (colf + ((kb - qb) * tq).astype(F32))
        if masked:
            r = lax.broadcasted_iota(I32, (2 * tq, tq), 0)
            c = lax.broadcasted_iota(I32, (2 * tq, tq), 1)
            u = jnp.where(c <= jnp.where(r >= tq, r - tq, r), u, -jnp.inf)
        m_old = m_ref[...]
        m_new = jnp.maximum(m_old, jnp.max(u, axis=1, keepdims=True))
        alpha = jnp.exp2(m_old - m_new)
        p = jnp.exp2(u - m_new)
        l_ref[...] = alpha * l_ref[...] + jnp.sum(p, axis=1, keepdims=True)
        acc_ref[...] = alpha * acc_ref[...] + _dot(p.astype(BF16), vt)
        m_ref[...] = m_new

    def loop_body(kb, carry):
        step(kb, False)
        return carry

    lax.fori_loop(0, qb, loop_body, 0)
    step(qb, True)

    lam_rows = lam_ref[...]
    lam = (jnp.exp(jnp.sum(lam_rows[0:1] * lam_rows[1:2], axis=1, keepdims=True))
           - jnp.exp(jnp.sum(lam_rows[2:3] * lam_rows[3:4], axis=1, keepdims=True)) + lam_init)
    acc = acc_ref[...]
    l = l_ref[...]
    o = acc[:tq] / l[:tq] - lam * (acc[tq:] / l[tq:])
    o_ref[...] = (_rms_rows(o, sn_ref[...]) * (1.0 - lam_init)).astype(BF16)


def _attention(q, k, v, slopes, lam_rows, sub_norm_row, batch, seq, lam_init):
    n, d = q.shape
    heads = d // LANE
    tq = min(TILES["tq"], seq)
    nq = seq // tq
    grid_spec = pltpu.PrefetchScalarGridSpec(
        num_scalar_prefetch=1,
        grid=(batch, heads, nq),
        in_specs=[
            pl.BlockSpec((tq, LANE), lambda b, h, i, s: (b * nq + i, h)),
            pl.BlockSpec((seq, LANE), lambda b, h, i, s: (b, h)),
            pl.BlockSpec((seq, LANE), lambda b, h, i, s: (b, h)),
            pl.BlockSpec((8, LANE), lambda b, h, i, s: (0, 0)),
            pl.BlockSpec((1, LANE), lambda b, h, i, s: (0, 0)),
        ],
        out_specs=pl.BlockSpec((tq, LANE), lambda b, h, i, s: (b * nq + i, h)),
        scratch_shapes=[pltpu.VMEM((2 * tq, 1), F32), pltpu.VMEM((2 * tq, 1), F32),
                        pltpu.VMEM((2 * tq, LANE), F32)],
    )
    return pl.pallas_call(
        functools.partial(_attn_kernel, tq=tq, lam_init=lam_init),
        grid_spec=grid_spec,
        out_shape=jax.ShapeDtypeStruct((n, d), BF16),
        compiler_params=_cparams(("arbitrary", "arbitrary", "arbitrary")),
        name="diff_attn",
    )(slopes, q, k, v, lam_rows, sub_norm_row)


def _proj_router_kernel(a_ref, w_ref, r_ref, g_ref, rw_ref, tri_ref, h_ref, hn_ref, meta_ref, cnt_ref,
                        run_ref, *, n_experts):
    @pl.when(pl.program_id(0) == 0)
    def _():
        run_ref[...] = jnp.zeros_like(run_ref)

    h = r_ref[...] + _dot(a_ref[...], w_ref[...])
    h_ref[...] = h
    hn = _rms_rows(h, g_ref[...])
    hn_ref[...] = hn
    logits = jnp.dot(hn, rw_ref[...], preferred_element_type=F32, precision=lax.Precision.HIGHEST)
    lane = lax.broadcasted_iota(I32, logits.shape, 1)
    lanef = lane.astype(F32)
    neg = jnp.float32(-jnp.inf)
    logits = jnp.where(lane < n_experts, logits, neg)
    v1 = jnp.max(logits, axis=1, keepdims=True)
    i1 = jnp.min(jnp.where(logits == v1, lanef, float(LANE)), axis=1, keepdims=True)
    oh1 = lanef == i1
    rest = jnp.where(oh1, neg, logits)
    v2 = jnp.max(rest, axis=1, keepdims=True)
    i2 = jnp.min(jnp.where(rest == v2, lanef, float(LANE)), axis=1, keepdims=True)
    oh2 = lanef == i2
    e = jnp.exp(v2 - v1)
    w1 = 1.0 / (1.0 + e)
    w2 = e * w1
    cnt = oh1.astype(F32) + oh2.astype(F32)
    before = run_ref[...] + _dot(tri_ref[...], cnt.astype(BF16))
    r1 = jnp.sum(jnp.where(oh1, before, 0.0), axis=1, keepdims=True)
    r2 = jnp.sum(jnp.where(oh2, before, 0.0), axis=1, keepdims=True)
    meta = jnp.where(lane == 0, i1, jnp.where(lane == 1, i2, jnp.where(lane == 2, w1, jnp.where(
        lane == 3, w2, jnp.where(lane == 4, r1, jnp.where(lane == 5, r2, 0.0))))))
    meta_ref[...] = meta
    run = run_ref[...] + jnp.sum(cnt, axis=0, keepdims=True)
    run_ref[...] = run
    cnt_ref[...] = run


def _proj_router(a, w_bf16, res, gain, router):
    n, d = res.shape
    n_experts = router.shape[1]
    tm = min(TILES["tm_router"], n)
    rw = jnp.zeros((d, LANE), F32).at[:, :n_experts].set(router.astype(F32))
    tri = jnp.tril(jnp.ones((tm, tm), F32), -1).astype(BF16)
    row = pl.BlockSpec((tm, d), lambda i: (i, 0))
    return pl.pallas_call(
        functools.partial(_proj_router_kernel, n_experts=n_experts),
        grid=(n // tm,),
        in_specs=[row, pl.BlockSpec((d, d), lambda i: (0, 0)), row,
                  pl.BlockSpec((1, d), lambda i: (0, 0)),
                  pl.BlockSpec((d, LANE), lambda i: (0, 0)),
                  pl.BlockSpec((tm, tm), lambda i: (0, 0))],
        out_specs=[row, row, pl.BlockSpec((tm, LANE), lambda i: (i, 0)),
                   pl.BlockSpec((1, LANE), lambda i: (0, 0))],
        out_shape=[jax.ShapeDtypeStruct((n, d), F32), jax.ShapeDtypeStruct((n, d), F32),
                   jax.ShapeDtypeStruct((n, LANE), F32), jax.ShapeDtypeStruct((1, LANE), F32)],
        scratch_shapes=[pltpu.VMEM((1, LANE), F32)],
        compiler_params=_cparams(("arbitrary",)),
        name="proj_router",
    )(a, w_bf16, res, gain, rw, tri)


def _scatter_kernel(pos_ref, hn_ref, xs_in_ref, xs_ref, sem, *, ts):
    del xs_in_ref

    def row_copy(r, p):
        return pltpu.make_async_copy(hn_ref.at[pl.ds(r, 1)], xs_ref.at[pl.ds(p, 1)], sem)

    def issue(r, carry):
        for kk in range(TOP_K):
            row_copy(r, pos_ref[0, 0, TOP_K * r + kk]).start()
        return carry

    def drain(r, carry):
        for kk in range(TOP_K):
            row_copy(r, pos_ref[0, 0, TOP_K * r + kk]).wait()
        return carry

    lax.fori_loop(0, ts, issue, 0)
    lax.fori_loop(0, ts, drain, 0)


def _scatter_rows(hn, pos, n_sorted):
    n, d = hn.shape
    ts = min(TILES["ts"], n)
    pos3 = pos.reshape(n // ts, 1, TOP_K * ts)
    return pl.pallas_call(
        functools.partial(_scatter_kernel, ts=ts),
        grid=(n // ts,),
        in_specs=[pl.BlockSpec((1, 1, TOP_K * ts), lambda i: (i, 0, 0), memory_space=pltpu.SMEM),
                  pl.BlockSpec((ts, d), lambda i: (i, 0)),
                  pl.BlockSpec(memory_space=pl.ANY)],
        out_specs=pl.BlockSpec(memory_space=pl.ANY),
        out_shape=jax.ShapeDtypeStruct((n_sorted, d), F32),
        scratch_shapes=[pltpu.SemaphoreType.DMA(())],
        input_output_aliases={2: 0},
        compiler_params=pltpu.CompilerParams(dimension_semantics=("arbitrary",),
                                             vmem_limit_bytes=VMEM_LIMIT, has_side_effects=True),
        name="moe_scatter",
    )(pos3, hn, jnp.zeros((n_sorted, d), F32))


def _expert_ffn_kernel(te_ref, na_ref, x_ref, wg_ref, wu_ref, wd_ref, y_ref, xb_ref, acc_ref):
    del te_ref
    i = pl.program_id(0)
    j = pl.program_id(1)

    @pl.when(i < na_ref[0])
    def _():
        @pl.when(j == 0)
        def _():
            xb_ref[...] = x_ref[...].astype(BF16)
            acc_ref[...] = jnp.zeros_like(acc_ref)

        xb = xb_ref[...]
        act = (_silu(_dot(xb, wg_ref[...])) * _dot(xb, wu_ref[...])).astype(BF16)
        acc_ref[...] += _dot(act, wd_ref[...])

        @pl.when(j == pl.num_programs(1) - 1)
        def _():
            y_ref[...] = acc_ref[...]

    @pl.when((i >= na_ref[0]) & (j == pl.num_programs(1) - 1))
    def _():
        y_ref[...] = jnp.zeros_like(y_ref)


def _expert_ffn(xs, tile_expert, n_active, w_gate_up_bf16, w_down_bf16, tmx):
    n_sorted, d = xs.shape
    f = w_down_bf16.shape[1]
    fb = _hidden_block(f, TILES["fbx"])
    nf = f // fb
    n_tiles = n_sorted // tmx

    def row_map(i, j, te, na):
        return (jnp.minimum(i, na[0] - 1), 0)

    def col(i, j, na):
        return jnp.where(i < na[0], j, nf - 1)

    grid_spec = pltpu.PrefetchScalarGridSpec(
        num_scalar_prefetch=2,
        grid=(n_tiles, nf),
        in_specs=[
            pl.BlockSpec((tmx, d), row_map),
            pl.BlockSpec((None, d, fb), lambda i, j, te, na: (te[i], 0, col(i, j, na))),
            pl.BlockSpec((None, d, fb), lambda i, j, te, na: (te[i], 0, col(i, j, na) + nf)),
            pl.BlockSpec((None, fb, d), lambda i, j, te, na: (te[i], col(i, j, na), 0)),
        ],
        out_specs=pl.BlockSpec((tmx, d), lambda i, j, te, na: (i, 0)),
        scratch_shapes=[pltpu.VMEM((tmx, d), BF16), pltpu.VMEM((tmx, d), F32)],
    )
    return pl.pallas_call(
        _expert_ffn_kernel,
        grid_spec=grid_spec,
        out_shape=jax.ShapeDtypeStruct((n_sorted, d), F32),
        compiler_params=_cparams(("arbitrary", "arbitrary")),
        name="expert_ffn",
    )(tile_expert, n_active, xs, w_gate_up_bf16, w_gate_up_bf16, w_down_bf16)


def _combine_kernel(pos_ref, h_ref, meta_ref, ys_ref, o_ref, buf_ref, sem, *, tc):
    def row_copy(r, kk, p):
        return pltpu.make_async_copy(ys_ref.at[pl.ds(p, 1)], buf_ref.at[kk, pl.ds(r, 1)], sem)

    def issue(r, carry):
        for kk in range(TOP_K):
            row_copy(r, kk, pos_ref[0, 0, TOP_K * r + kk]).start()
        return carry

    def drain(r, carry):
        for kk in range(TOP_K):
            row_copy(r, kk, pos_ref[0, 0, TOP_K * r + kk]).wait()
        return carry

    lax.fori_loop(0, tc, issue, 0)
    lax.fori_loop(0, tc, drain, 0)
    meta = meta_ref[...]
    o_ref[...] = h_ref[...] + meta[:, 2:3] * buf_ref[0] + meta[:, 3:4] * buf_ref[1]


def _combine(h, meta, pos, ys):
    n, d = h.shape
    tc = min(TILES["tc"], n)
    pos3 = pos.reshape(n // tc, 1, TOP_K * tc)
    return pl.pallas_call(
        functools.partial(_combine_kernel, tc=tc),
        grid=(n // tc,),
        in_specs=[pl.BlockSpec((1, 1, TOP_K * tc), lambda i: (i, 0, 0), memory_space=pltpu.SMEM),
                  pl.BlockSpec((tc, d), lambda i: (i, 0)),
                  pl.BlockSpec((tc, LANE), lambda i: (i, 0)),
                  pl.BlockSpec(memory_space=pl.ANY)],
        out_specs=pl.BlockSpec((tc, d), lambda i: (i, 0)),
        out_shape=jax.ShapeDtypeStruct((n, d), F32),
        scratch_shapes=[pltpu.VMEM((TOP_K, tc, d), F32), pltpu.SemaphoreType.DMA(())],
        compiler_params=_cparams(("arbitrary",)),
        name="moe_combine",
    )(pos3, h, meta, ys)


def _row(vec):
    return vec.astype(F32).reshape(1, -1)


def kernel(x, lower_bounds, l0_mix_norm, l0_hgrn_w_in, l0_hgrn_out_norm, l0_hgrn_w_out, l0_ffn_norm, l0_ffn_w_gate_up, l0_ffn_w_down, l1_mix_norm, l1_diff_w_in, l1_q_norm, l1_k_norm, l1_lambda_q1, l1_lambda_k1, l1_lambda_q2, l1_lambda_k2, l1_diff_sub_norm, l1_diff_w_out, l1_ffn_norm, l1_router, l1_moe_w_gate_up, l1_moe_w_down):
    batch, seq, d = x.shape
    n = batch * seq
    heads = d // LANE
    n_experts = l1_router.shape[1]
    x2 = x.reshape(n, d).astype(F32)

    lb0 = jnp.cumsum(jax.nn.softmax(lower_bounds.astype(F32), axis=0), axis=0)[0]
    q, k, lf, v, gs = _hgrn_in(x2, _row(l0_mix_norm), _row(lb0), l0_hgrn_w_in.astype(BF16))
    og = _hgrn_rec(q, k, lf, v, gs, _row(l0_hgrn_out_norm), batch, seq)
    h = _proj_res(og, l0_hgrn_w_out.astype(BF16), x2)
    h = _ffn(h, _row(l0_ffn_norm), l0_ffn_w_gate_up.astype(BF16), l0_ffn_w_down.astype(BF16))

    lam_init = 0.8 - 0.6 * math.exp(-0.3 * 1)
    q, k, v = _diff_in(h, _row(l1_mix_norm), l1_diff_w_in.astype(BF16),
                       _row(jnp.tile(l1_q_norm, 2 * heads)), _row(jnp.tile(l1_k_norm, 2 * heads)))
    slopes = 2.0 ** (-8.0 * jnp.arange(1, heads + 1, dtype=F32) / heads)
    lam_rows = jnp.zeros((8, LANE), F32).at[:4, :DIFF_HEAD_DIM].set(
        jnp.stack([l1_lambda_q1, l1_lambda_k1, l1_lambda_q2, l1_lambda_k2]).astype(F32))
    oa = _attention(q, k, v, slopes, lam_rows, _row(l1_diff_sub_norm), batch, seq, lam_init)

    h, hn, meta, counts = _proj_router(oa, l1_diff_w_out.astype(BF16), h, _row(l1_ffn_norm), l1_router)
    tmx = min(TILES["tmx"], n)
    n_tiles = (TOP_K * n) // tmx + n_experts
    cnt = counts[0, :n_experts].astype(I32)
    tiles_per = (cnt + tmx - 1) // tmx
    tile_end = jnp.cumsum(tiles_per)
    row_start = (tile_end - tiles_per) * tmx
    n_active = tile_end[-1:]
    tile_ids = jnp.arange(n_tiles, dtype=I32)
    tile_expert = jnp.sum((tile_ids[:, None] >= tile_end[None, :]).astype(I32), axis=1)
    tile_expert = jnp.minimum(tile_expert, tile_expert[n_active[0] - 1]).astype(I32)
    eid = meta[:, 0:TOP_K].astype(I32)
    pos = row_start[eid] + meta[:, 4:4 + TOP_K].astype(I32)
    xs = _scatter_rows(hn, pos, n_tiles * tmx)
    ys = _expert_ffn(xs, tile_expert, n_active.astype(I32), l1_moe_w_gate_up.astype(BF16),
                     l1_moe_w_down.astype(BF16), tmx)
    out = _combine(h, meta, pos, ys)
    return out.reshape(batch, seq, d).astype(x.dtype)
```

```python
import functools
import math

import jax
import jax.numpy as jnp
from jax import lax
from jax.experimental import pallas as pl
from jax.experimental.pallas import tpu as pltpu

F32 = jnp.float32
BF16 = jnp.bfloat16
I32 = jnp.int32

NORM_EPS = 1e-6
LANE = 128
HGRN_CHUNK = 64
DIFF_HEAD_DIM = 64
TOP_K = 2
LOG2E = math.log2(math.e)
EXP_CLAMP = 60.0
VMEM_LIMIT = 56 * 1024 * 1024

TILES = dict(
    tm_in=512,
    tb_rec=512,
    tm_proj=512,
    tm_ffn=512,
    tq=512,
    tm_router=512,
    tmx=1024,
    fbx=512,
)


def _cparams(sem):
    return pltpu.CompilerParams(dimension_semantics=sem, vmem_limit_bytes=VMEM_LIMIT)


def _sigmoid(x):
    return 1.0 / (1.0 + jnp.exp(-x))


def _silu(x):
    return x * _sigmoid(x)


def _rms_rows(x, gain):
    ms = jnp.mean(x * x, axis=-1, keepdims=True)
    return x * lax.rsqrt(ms + NORM_EPS) * gain


def _dot(a, b):
    return jnp.dot(a, b, preferred_element_type=F32)


def _dot_nt(a, b):
    return lax.dot_general(a, b, (((1,), (1,)), ((), ())), preferred_element_type=F32)


def _dot_tn(a, b):
    return lax.dot_general(a, b, (((0,), (0,)), ((), ())), preferred_element_type=F32)


def _hgrn_in_kernel(x_ref, g_ref, lb_ref, w_ref, q_ref, k_ref, lf_ref, v_ref, gs_ref):
    d = x_ref.shape[1]
    xn = _rms_rows(x_ref[...], g_ref[...]).astype(BF16)

    def proj(j):
        return _dot(xn, w_ref[:, j * d:(j + 1) * d])

    q_ref[...] = _silu(proj(0)).astype(BF16)
    lb = lb_ref[...]
    forget = lb + (1.0 - lb) * _sigmoid(proj(1))
    k_ref[...] = (1.0 - forget).astype(BF16)
    lf_ref[...] = jnp.log(forget)
    v_ref[...] = proj(2).astype(BF16)
    gs_ref[...] = _silu(proj(3)).astype(BF16)


def _hgrn_in(x, gain, lb, w_bf16):
    n, d = x.shape
    tm = min(TILES["tm_in"], n)
    row = pl.BlockSpec((tm, d), lambda i: (i, 0))
    vec = pl.BlockSpec((1, d), lambda i: (0, 0))
    return pl.pallas_call(
        _hgrn_in_kernel,
        grid=(n // tm,),
        in_specs=[row, vec, vec, pl.BlockSpec((d, 4 * d), lambda i: (0, 0))],
        out_specs=[row, row, row, row, row],
        out_shape=[
            jax.ShapeDtypeStruct((n, d), BF16),
            jax.ShapeDtypeStruct((n, d), BF16),
            jax.ShapeDtypeStruct((n, d), F32),
            jax.ShapeDtypeStruct((n, d), BF16),
            jax.ShapeDtypeStruct((n, d), BF16),
        ],
        compiler_params=_cparams(("arbitrary",)),
        name="hgrn_in",
    )(x, gain, lb, w_bf16)


def _hgrn_rec_kernel(q_ref, k_ref, lf_ref, v_ref, gs_ref, on_ref, o_ref, st_ref, *, chunk, heads):
    @pl.when(pl.program_id(1) == 0)
    def _():
        st_ref[...] = jnp.zeros_like(st_ref)

    c = chunk
    n_chunks = q_ref.shape[0] // c
    row = lax.broadcasted_iota(I32, (c, c), 0)
    col = lax.broadcasted_iota(I32, (c, c), 1)
    causal = col <= row
    tri = causal.astype(BF16)
    mid = c // 2 - 1

    def body(ci, carry):
        r0 = pl.multiple_of(ci * c, c)
        rows = pl.ds(r0, c)
        lf = lf_ref[rows, :]
        hi = lf.astype(BF16)
        lo = (lf - hi.astype(F32)).astype(BF16)
        b = _dot(tri, hi) + _dot(tri, lo)
        bmid = b[mid:mid + 1, :]
        bend = b[c - 1:c, :]
        e_q = jnp.exp(jnp.minimum(b - bmid, EXP_CLAMP))
        e_k = jnp.exp(jnp.minimum(bmid - b, EXP_CLAMP))
        qt = (q_ref[rows, :].astype(F32) * e_q).astype(BF16)
        kt = (k_ref[rows, :].astype(F32) * e_k).astype(BF16)
        v = v_ref[rows, :]
        gs = gs_ref[rows, :].astype(F32)
        e_mid = jnp.exp(bmid)
        e_end = jnp.exp(bend)
        e_end_mid = jnp.exp(bend - bmid)
        for h in range(heads):
            sl = slice(h * LANE, (h + 1) * LANE)
            qh, kh, vh = qt[:, sl], kt[:, sl], v[:, sl]
            a = jnp.where(causal, _dot_nt(qh, kh), 0.0).astype(BF16)
            st = st_ref[h]
            o = _dot(a, vh) + _dot_nt(qh, (st * e_mid[:, sl]).astype(BF16))
            st_ref[h] = st * e_end[:, sl] + _dot_tn(vh, kh) * e_end_mid[:, sl]
            og = _rms_rows(o, on_ref[:, sl]) * gs[:, sl]
            o_ref[rows, sl] = og.astype(BF16)
        return carry

    lax.fori_loop(0, n_chunks, body, 0, unroll=2)


def _hgrn_rec(q, k, lf, v, gs, out_norm, batch, seq):
    n, d = q.shape
    heads = d // LANE
    tb = min(TILES["tb_rec"], seq)
    nt = seq // tb
    blk = pl.BlockSpec((tb, d), lambda b, t: (b * nt + t, 0))
    return pl.pallas_call(
        functools.partial(_hgrn_rec_kernel, chunk=HGRN_CHUNK, heads=heads),
        grid=(batch, nt),
        in_specs=[blk, blk, blk, blk, blk, pl.BlockSpec((1, d), lambda b, t: (0, 0))],
        out_specs=blk,
        out_shape=jax.ShapeDtypeStruct((n, d), BF16),
        scratch_shapes=[pltpu.VMEM((heads, LANE, LANE), F32)],
        compiler_params=_cparams(("arbitrary", "arbitrary")),
        name="hgrn_rec",
    )(q, k, lf, v, gs, out_norm)


def _proj_res_kernel(a_ref, w_ref, r_ref, o_ref):
    o_ref[...] = r_ref[...] + _dot(a_ref[...], w_ref[...])


def _proj_res(a, w_bf16, res):
    n, d = res.shape
    tm = min(TILES["tm_proj"], n)
    row = pl.BlockSpec((tm, d), lambda i: (i, 0))
    return pl.pallas_call(
        _proj_res_kernel,
        grid=(n // tm,),
        in_specs=[row, pl.BlockSpec((d, d), lambda i: (0, 0)), row],
        out_specs=row,
        out_shape=jax.ShapeDtypeStruct((n, d), F32),
        compiler_params=_cparams(("arbitrary",)),
        name="proj_res",
    )(a, w_bf16, res)


def _ffn_kernel(h_ref, g_ref, wgu_ref, wd_ref, o_ref):
    f = wd_ref.shape[0]
    h = h_ref[...]
    xn = _rms_rows(h, g_ref[...]).astype(BF16)
    gu = _dot(xn, wgu_ref[...])
    act = (_silu(gu[:, :f]) * gu[:, f:]).astype(BF16)
    o_ref[...] = h + _dot(act, wd_ref[...])


def _hidden_block(f, want):
    best = None
    for fb in range(LANE, f + 1, LANE):
        if f % fb == 0 and fb <= want:
            best = fb
    assert best is not None, f
    return best


def _ffn(h, gain, w_gate_up_bf16, w_down_bf16):
    n, d = h.shape
    f = w_down_bf16.shape[0]
    tm = min(TILES["tm_ffn"], n)
    row = pl.BlockSpec((tm, d), lambda i: (i, 0))
    once = pl.Buffered(1)
    return pl.pallas_call(
        _ffn_kernel,
        grid=(n // tm,),
        in_specs=[
            row,
            pl.BlockSpec((1, d), lambda i: (0, 0)),
            pl.BlockSpec((d, 2 * f), lambda i: (0, 0), pipeline_mode=once),
            pl.BlockSpec((f, d), lambda i: (0, 0), pipeline_mode=once),
        ],
        out_specs=row,
        out_shape=jax.ShapeDtypeStruct((n, d), F32),
        compiler_params=_cparams(("arbitrary",)),
        name="ffn",
    )(h, gain, w_gate_up_bf16, w_down_bf16)


def _group_rms(y, gmat, gain):
    d = y.shape[1]
    sq = (y * y).astype(BF16)
    ms = jnp.concatenate(
        [_dot(sq[:, s * LANE:(s + 1) * LANE], gmat) for s in range(d // LANE)], axis=1)
    return y * lax.rsqrt(ms + NORM_EPS) * gain


def _diff_in_kernel(x_ref, g_ref, w_ref, gm_ref, qn_ref, kn_ref, q_ref, k_ref, v_ref):
    d = x_ref.shape[1]
    xn = _rms_rows(x_ref[...], g_ref[...]).astype(BF16)

    def proj(j):
        return _dot(xn, w_ref[:, j * d:(j + 1) * d])

    q_ref[...] = (_group_rms(proj(0), gm_ref[...], qn_ref[...]) * (DIFF_HEAD_DIM ** -0.5 * LOG2E)).astype(BF16)
    k_ref[...] = _group_rms(proj(1), gm_ref[...], kn_ref[...]).astype(BF16)
    v_ref[...] = proj(2).astype(BF16)


def _diff_in(x, gain, w_bf16, q_norm_row, k_norm_row):
    n, d = x.shape
    tm = min(TILES["tm_in"], n)
    lane_group = jnp.arange(LANE) // DIFF_HEAD_DIM
    gmat = ((lane_group[:, None] == lane_group[None, :]).astype(F32) / DIFF_HEAD_DIM).astype(BF16)
    row = pl.BlockSpec((tm, d), lambda i: (i, 0))
    vec = pl.BlockSpec((1, d), lambda i: (0, 0))
    return pl.pallas_call(
        _diff_in_kernel,
        grid=(n // tm,),
        in_specs=[row, vec, pl.BlockSpec((d, 3 * d), lambda i: (0, 0)),
                  pl.BlockSpec((LANE, LANE), lambda i: (0, 0)), vec, vec],
        out_specs=[row, row, row],
        out_shape=[jax.ShapeDtypeStruct((n, d), BF16)] * 3,
        compiler_params=_cparams(("arbitrary",)),
        name="diff_in",
    )(x, gain, w_bf16, gmat, q_norm_row, k_norm_row)


V_ROWS = LANE + 16


def _attn_kernel(slope_ref, q_ref, k_ref, vt_ref, lam_ref, sn_ref, o_ref, m_ref, acc_ref, w_ref,
                 *, tq, lam_init):
    h = pl.program_id(1)
    qb = pl.program_id(2)
    slope = slope_ref[h] * LOG2E

    q = q_ref[...]
    lane = lax.broadcasted_iota(I32, q.shape, 1)
    zero = jnp.zeros_like(q)
    qs = jnp.concatenate([jnp.where(lane < DIFF_HEAD_DIM, q, zero),
                          jnp.where(lane >= DIFF_HEAD_DIM, q, zero)], axis=0)

    m_ref[...] = jnp.full_like(m_ref, -jnp.inf)
    acc_ref[...] = jnp.zeros_like(acc_ref)

    lane_k = lax.broadcasted_iota(I32, (tq, LANE), 1)
    bias = slope * lax.broadcasted_iota(I32, (tq, LANE), 0).astype(F32)
    bias_hi = bias.astype(BF16).astype(F32)
    bias_cols = jnp.where(lane_k == 0, bias_hi, jnp.where(lane_k == 1, bias - bias_hi, 0.0)).astype(BF16)
    lane_q = lax.broadcasted_iota(I32, (2 * tq, LANE), 1)
    qsx = jnp.concatenate([qs, jnp.where(lane_q < 2, 1.0, 0.0).astype(BF16)], axis=1)

    def scores(kb):
        k0 = pl.multiple_of(kb * tq, tq)
        return _dot_nt(jnp.concatenate([k_ref[pl.ds(k0, tq), :], bias_cols], axis=1), qsx)

    def consume(kb, w, masked):
        if masked:
            r = lax.broadcasted_iota(I32, w.shape, 0)
            c = lax.broadcasted_iota(I32, w.shape, 1)
            w = jnp.where(r <= jnp.where(c >= tq, c - tq, c), w, -jnp.inf)
        off = slope * ((kb - qb) * tq).astype(F32)
        m_old = m_ref[...]
        m_new = jnp.maximum(m_old, jnp.max(w, axis=0, keepdims=True) + off)
        alpha = jnp.exp2(m_old - m_new)
        p = jnp.exp2(w - (m_new - off)).astype(BF16)
        k0 = pl.multiple_of(kb * tq, tq)
        acc_ref[...] = alpha * acc_ref[...] + _dot(vt_ref[:, pl.ds(k0, tq)], p)
        m_ref[...] = m_new

    w_ref[0] = scores(0)

    def pair(j, carry):
        kb = 2 * j
        w_ref[1] = scores(kb + 1)
        consume(kb, w_ref[0], False)
        w_ref[0] = scores(kb + 2)
        consume(kb + 1, w_ref[1], False)
        return carry

    lax.fori_loop(0, qb // 2, pair, 0)

    @pl.when(qb % 2 == 0)
    def _():
        consume(qb, w_ref[0], True)

    @pl.when(qb % 2 == 1)
    def _():
        w_ref[1] = scores(qb)
        consume(qb - 1, w_ref[0], False)
        consume(qb, w_ref[1], True)

    lam_rows = lam_ref[...]
    lam = (jnp.exp(jnp.sum(lam_rows[0:1] * lam_rows[1:2], axis=1, keepdims=True))
           - jnp.exp(jnp.sum(lam_rows[2:3] * lam_rows[3:4], axis=1, keepdims=True)) + lam_init)
    acc = acc_ref[...]
    on = acc[:LANE] / acc[LANE:LANE + 1]
    o = (on[:, :tq] - lam * on[:, tq:]).T
    o_ref[...] = (_rms_rows(o, sn_ref[...]) * (1.0 - lam_init)).astype(BF16)


def _attention(q, k, v, slopes, lam_rows, sub_norm_row, batch, seq, lam_init):
    n, d = q.shape
    heads = d // LANE
    tq = min(TILES["tq"], seq)
    nq = seq // tq
    vt = v.reshape(batch, seq, heads, LANE).transpose(0, 2, 3, 1)
    vt = jnp.concatenate([vt, jnp.ones((batch, heads, V_ROWS - LANE, seq), BF16)], axis=2)
    grid_spec = pltpu.PrefetchScalarGridSpec(
        num_scalar_prefetch=1,
        grid=(batch, heads, nq),
        in_specs=[
            pl.BlockSpec((tq, LANE), lambda b, h, i, s: (b * nq + i, h)),
            pl.BlockSpec((seq, LANE), lambda b, h, i, s: (b, h)),
            pl.BlockSpec((None, None, V_ROWS, seq), lambda b, h, i, s: (b, h, 0, 0)),
            pl.BlockSpec((8, LANE), lambda b, h, i, s: (0, 0)),
            pl.BlockSpec((1, LANE), lambda b, h, i, s: (0, 0)),
        ],
        out_specs=pl.BlockSpec((tq, LANE), lambda b, h, i, s: (b * nq + i, h)),
        scratch_shapes=[pltpu.VMEM((1, 2 * tq), F32), pltpu.VMEM((V_ROWS, 2 * tq), F32),
                        pltpu.VMEM((2, tq, 2 * tq), F32)],
    )
    return pl.pallas_call(
        functools.partial(_attn_kernel, tq=tq, lam_init=lam_init),
        grid_spec=grid_spec,
        out_shape=jax.ShapeDtypeStruct((n, d), BF16),
        compiler_params=_cparams(("arbitrary", "arbitrary", "arbitrary")),
        name="diff_attn",
    )(slopes, q, k, vt, lam_rows, sub_norm_row)


ROUTE_ROWS = 16
ROW_ALIGN = 8


def _proj_router_kernel(a_ref, w_ref, r_ref, g_ref, rw_ref, triu_ref, h_ref, hn_ref, meta_ref, cnt_ref,
                        *, n_experts):
    h = r_ref[...] + _dot(a_ref[...], w_ref[...])
    h_ref[...] = h
    hn = _rms_rows(h, g_ref[...])
    hn_hi = hn.astype(BF16)
    hn_ref[...] = hn_hi
    hn_lo = (hn - hn_hi.astype(F32)).astype(BF16)
    both = _dot(hn_hi, rw_ref[...])
    logits = both[:, :LANE] + both[:, LANE:] + _dot(hn_lo, rw_ref[:, :LANE])
    lt = logits.T[:ROUTE_ROWS]
    sub = lax.broadcasted_iota(I32, lt.shape, 0)
    subf = sub.astype(F32)
    neg = jnp.float32(-jnp.inf)
    lt = jnp.where(sub < n_experts, lt, neg)
    v1 = jnp.max(lt, axis=0, keepdims=True)
    i1 = jnp.min(jnp.where(lt == v1, subf, float(ROUTE_ROWS)), axis=0, keepdims=True)
    oh1 = subf == i1
    rest = jnp.where(oh1, neg, lt)
    v2 = jnp.max(rest, axis=0, keepdims=True)
    i2 = jnp.min(jnp.where(rest == v2, subf, float(ROUTE_ROWS)), axis=0, keepdims=True)
    oh2 = subf == i2
    e = jnp.exp(v2 - v1)
    w1 = 1.0 / (1.0 + e)
    w2 = e * w1
    cnt = jnp.where(oh1, 1.0, 0.0) + jnp.where(oh2, 1.0, 0.0)
    before = _dot(cnt.astype(BF16), triu_ref[...])
    r1 = jnp.sum(jnp.where(oh1, before, 0.0), axis=0, keepdims=True)
    r2 = jnp.sum(jnp.where(oh2, before, 0.0), axis=0, keepdims=True)
    meta = jnp.where(sub == 0, i1, jnp.where(sub == 1, i2, jnp.where(sub == 2, w1, jnp.where(
        sub == 3, w2, jnp.where(sub == 4, r1, jnp.where(sub == 5, r2, 0.0))))))
    meta_ref[...] = meta[:8]
    cnt_ref[...] = jnp.sum(cnt, axis=1, keepdims=True) + jnp.zeros((ROUTE_ROWS, LANE), F32)


def _proj_router(a, w_bf16, res, gain, router):
    n, d = res.shape
    n_experts = router.shape[1]
    assert n_experts <= ROUTE_ROWS
    tm = min(TILES["tm_router"], n)
    rw = jnp.zeros((d, LANE), F32).at[:, :n_experts].set(router.astype(F32))
    rw_hi = rw.astype(BF16)
    rw_cat = jnp.concatenate([rw_hi, (rw - rw_hi.astype(F32)).astype(BF16)], axis=1)
    triu = jnp.triu(jnp.ones((tm, tm), F32), 1).astype(BF16)
    row = pl.BlockSpec((tm, d), lambda i: (i, 0))
    return pl.pallas_call(
        functools.partial(_proj_router_kernel, n_experts=n_experts),
        grid=(n // tm,),
        in_specs=[row, pl.BlockSpec((d, d), lambda i: (0, 0)), row,
                  pl.BlockSpec((1, d), lambda i: (0, 0)),
                  pl.BlockSpec((d, 2 * LANE), lambda i: (0, 0)),
                  pl.BlockSpec((tm, tm), lambda i: (0, 0))],
        out_specs=[row, row, pl.BlockSpec((8, tm), lambda i: (0, i)),
                   pl.BlockSpec((ROUTE_ROWS, LANE), lambda i: (i, 0))],
        out_shape=[jax.ShapeDtypeStruct((n, d), F32), jax.ShapeDtypeStruct((n, d), BF16),
                   jax.ShapeDtypeStruct((8, n), F32),
                   jax.ShapeDtypeStruct((n // tm * ROUTE_ROWS, LANE), F32)],
        compiler_params=_cparams(("arbitrary",)),
        name="proj_router",
    )(a, w_bf16, res, gain, rw_cat, triu)


def _segment_copies(pc_ref, seg_ref, dst_ref, tile, n_experts, max_rows, make_copy, start):
    sizes = []
    size = ROW_ALIGN
    while size <= max_rows:
        sizes.append(size)
        size *= 2
    for e in range(n_experts):
        idx = tile * n_experts + e
        n_rows = pc_ref[idx]
        local0 = seg_ref[idx]
        sorted0 = dst_ref[idx]
        done = 0
        for size in reversed(sizes):
            take = (n_rows & size) != 0
            local = pl.multiple_of(local0 + done, ROW_ALIGN)
            glob = pl.multiple_of(sorted0 + done, ROW_ALIGN)

            @pl.when(take)
            def _(local=local, glob=glob, size=size):
                cp = make_copy(local, glob, size)
                if start:
                    cp.start()
                else:
                    cp.wait()

            done = done + jnp.where(take, size, 0)


def _slot_one_hot(slot_ref, n_slots, values=None):
    tokens = slot_ref.shape[1]
    sub = lax.broadcasted_iota(I32, (n_slots, tokens), 0)
    out = None
    for kk in range(TOP_K):
        val = 1.0 if values is None else values[kk]
        term = jnp.where(sub == slot_ref[kk:kk + 1, :], val, 0.0)
        out = term if out is None else out + term
    return out.astype(BF16)


def _scatter_kernel(pc_ref, seg_ref, dst_ref, slot_ref, hn_ref, xs_in_ref, xs_ref, sorted_ref, sem,
                    *, n_experts, n_slots):
    del xs_in_ref
    tile = pl.program_id(0)
    sorted_ref[...] = _dot(_slot_one_hot(slot_ref, n_slots), hn_ref[...])

    def make_copy(local, glob, size):
        return pltpu.make_async_copy(sorted_ref.at[pl.ds(local, size)], xs_ref.at[pl.ds(glob, size)], sem)

    max_rows = hn_ref.shape[0]
    _segment_copies(pc_ref, seg_ref, dst_ref, tile, n_experts, max_rows, make_copy, True)
    _segment_copies(pc_ref, seg_ref, dst_ref, tile, n_experts, max_rows, make_copy, False)


def _scatter_rows(hn, slots, pc, seg, dst, n_sorted, tm, n_experts):
    n, d = hn.shape
    n_slots = TOP_K * tm + ROW_ALIGN * n_experts
    grid_spec = pltpu.PrefetchScalarGridSpec(
        num_scalar_prefetch=3,
        grid=(n // tm,),
        in_specs=[pl.BlockSpec((8, tm), lambda i, *_: (0, i)),
                  pl.BlockSpec((tm, d), lambda i, *_: (i, 0)),
                  pl.BlockSpec(memory_space=pl.ANY)],
        out_specs=pl.BlockSpec(memory_space=pl.ANY),
        scratch_shapes=[pltpu.VMEM((n_slots, d), F32), pltpu.SemaphoreType.DMA(())],
    )
    return pl.pallas_call(
        functools.partial(_scatter_kernel, n_experts=n_experts, n_slots=n_slots),
        grid_spec=grid_spec,
        out_shape=jax.ShapeDtypeStruct((n_sorted, d), F32),
        input_output_aliases={5: 0},
        compiler_params=pltpu.CompilerParams(dimension_semantics=("arbitrary",),
                                             vmem_limit_bytes=VMEM_LIMIT, has_side_effects=True),
        name="moe_scatter",
    )(pc, seg, dst, slots, hn, jnp.zeros((n_sorted, d), F32))


EXPERT_SUB_ROWS = 512


def _expert_ffn_kernel(te_ref, na_ref, x_ref, wg_ref, wu_ref, wd_ref, y_ref, xb_ref, acc_ref):
    del te_ref
    i = pl.program_id(0)
    j = pl.program_id(1)

    @pl.when(i < na_ref[0])
    def _():
        @pl.when(j == 0)
        def _():
            xb_ref[...] = x_ref[...].astype(BF16)
            acc_ref[...] = jnp.zeros_like(acc_ref)

        sub_rows = min(EXPERT_SUB_ROWS, xb_ref.shape[0])
        for r0 in range(0, xb_ref.shape[0], sub_rows):
            rows = slice(r0, r0 + sub_rows)
            xb = xb_ref[rows, :]
            act = (_silu(_dot(xb, wg_ref[...])) * _dot(xb, wu_ref[...])).astype(BF16)
            acc_ref[rows, :] += _dot(act, wd_ref[...])

        @pl.when(j == pl.num_programs(1) - 1)
        def _():
            y_ref[...] = acc_ref[...]

    @pl.when((i >= na_ref[0]) & (j == pl.num_programs(1) - 1))
    def _():
        y_ref[...] = jnp.zeros_like(y_ref)


def _expert_ffn(xs, tile_expert, n_active, w_gate_up_bf16, w_down_bf16, tmx):
    n_sorted, d = xs.shape
    f = w_down_bf16.shape[1]
    fb = _hidden_block(f, TILES["fbx"])
    nf = f // fb
    n_tiles = n_sorted // tmx

    def row_map(i, j, te, na):
        return (jnp.minimum(i, na[0] - 1), 0)

    def col(i, j, na):
        return jnp.where(i < na[0], j, nf - 1)

    grid_spec = pltpu.PrefetchScalarGridSpec(
        num_scalar_prefetch=2,
        grid=(n_tiles, nf),
        in_specs=[
            pl.BlockSpec((tmx, d), row_map),
            pl.BlockSpec((None, d, fb), lambda i, j, te, na: (te[i], 0, col(i, j, na))),
            pl.BlockSpec((None, d, fb), lambda i, j, te, na: (te[i], 0, col(i, j, na) + nf)),
            pl.BlockSpec((None, fb, d), lambda i, j, te, na: (te[i], col(i, j, na), 0)),
        ],
        out_specs=pl.BlockSpec((tmx, d), lambda i, j, te, na: (i, 0)),
        scratch_shapes=[pltpu.VMEM((tmx, d), BF16), pltpu.VMEM((tmx, d), F32)],
    )
    return pl.pallas_call(
        _expert_ffn_kernel,
        grid_spec=grid_spec,
        out_shape=jax.ShapeDtypeStruct((n_sorted, d), F32),
        compiler_params=_cparams(("arbitrary", "arbitrary")),
        name="expert_ffn",
    )(tile_expert, n_active, xs, w_gate_up_bf16, w_gate_up_bf16, w_down_bf16)


def _combine_kernel(pc_ref, seg_ref, dst_ref, slot_ref, meta_ref, h_ref, ys_ref, o_ref, buf_ref, sem,
                    *, n_experts, n_slots):
    tile = pl.program_id(0)

    @pl.when(tile == 0)
    def _():
        buf_ref[...] = jnp.zeros_like(buf_ref)

    def make_copy(local, glob, size):
        return pltpu.make_async_copy(ys_ref.at[pl.ds(glob, size)], buf_ref.at[pl.ds(local, size)], sem)

    max_rows = h_ref.shape[0]
    _segment_copies(pc_ref, seg_ref, dst_ref, tile, n_experts, max_rows, make_copy, True)
    gates = _slot_one_hot(slot_ref, n_slots, [meta_ref[2 + kk:3 + kk, :] for kk in range(TOP_K)])
    _segment_copies(pc_ref, seg_ref, dst_ref, tile, n_experts, max_rows, make_copy, False)
    o_ref[...] = h_ref[...] + _dot_tn(gates, buf_ref[...].astype(BF16))


def _combine(h, meta, slots, pc, seg, dst, ys, tm, n_experts):
    n, d = h.shape
    n_slots = TOP_K * tm + ROW_ALIGN * n_experts
    grid_spec = pltpu.PrefetchScalarGridSpec(
        num_scalar_prefetch=3,
        grid=(n // tm,),
        in_specs=[pl.BlockSpec((8, tm), lambda i, *_: (0, i)),
                  pl.BlockSpec((8, tm), lambda i, *_: (0, i)),
                  pl.BlockSpec((tm, d), lambda i, *_: (i, 0)),
                  pl.BlockSpec(memory_space=pl.ANY)],
        out_specs=pl.BlockSpec((tm, d), lambda i, *_: (i, 0)),
        scratch_shapes=[pltpu.VMEM((n_slots, d), F32), pltpu.SemaphoreType.DMA(())],
    )
    return pl.pallas_call(
        functools.partial(_combine_kernel, n_experts=n_experts, n_slots=n_slots),
        grid_spec=grid_spec,
        out_shape=jax.ShapeDtypeStruct((n, d), F32),
        compiler_params=_cparams(("arbitrary",)),
        name="moe_combine",
    )(pc, seg, dst, slots, meta, h, ys)


def _row(vec):
    return vec.astype(F32).reshape(1, -1)


def _round_up(x, m):
    return (x + m - 1) // m * m


def kernel(x, lower_bounds, l0_mix_norm, l0_hgrn_w_in, l0_hgrn_out_norm, l0_hgrn_w_out, l0_ffn_norm, l0_ffn_w_gate_up, l0_ffn_w_down, l1_mix_norm, l1_diff_w_in, l1_q_norm, l1_k_norm, l1_lambda_q1, l1_lambda_k1, l1_lambda_q2, l1_lambda_k2, l1_diff_sub_norm, l1_diff_w_out, l1_ffn_norm, l1_router, l1_moe_w_gate_up, l1_moe_w_down):
    batch, seq, d = x.shape
    n = batch * seq
    heads = d // LANE
    n_experts = l1_router.shape[1]
    x2 = x.reshape(n, d).astype(F32)

    lb0 = jnp.cumsum(jax.nn.softmax(lower_bounds.astype(F32), axis=0), axis=0)[0]
    q, k, lf, v, gs = _hgrn_in(x2, _row(l0_mix_norm), _row(lb0), l0_hgrn_w_in.astype(BF16))
    og = _hgrn_rec(q, k, lf, v, gs, _row(l0_hgrn_out_norm), batch, seq)
    h = _proj_res(og, l0_hgrn_w_out.astype(BF16), x2)
    h = _ffn(h, _row(l0_ffn_norm), l0_ffn_w_gate_up.astype(BF16), l0_ffn_w_down.astype(BF16))

    lam_init = 0.8 - 0.6 * math.exp(-0.3 * 1)
    q, k, v = _diff_in(h, _row(l1_mix_norm), l1_diff_w_in.astype(BF16),
                       _row(jnp.tile(l1_q_norm, 2 * heads)), _row(jnp.tile(l1_k_norm, 2 * heads)))
    slopes = 2.0 ** (-8.0 * jnp.arange(1, heads + 1, dtype=F32) / heads)
    lam_rows = jnp.zeros((8, LANE), F32).at[:4, :DIFF_HEAD_DIM].set(
        jnp.stack([l1_lambda_q1, l1_lambda_k1, l1_lambda_q2, l1_lambda_k2]).astype(F32))
    oa = _attention(q, k, v, slopes, lam_rows, _row(l1_diff_sub_norm), batch, seq, lam_init)

    h, hn, meta, tile_cnt = _proj_router(oa, l1_diff_w_out.astype(BF16), h, _row(l1_ffn_norm), l1_router)
    tm = min(TILES["tm_router"], n)
    n_tok_tiles = n // tm
    tmx = min(TILES["tmx"], n)
    cnt = tile_cnt.reshape(n_tok_tiles, ROUTE_ROWS, LANE)[:, :n_experts, 0].astype(I32)
    pc = _round_up(cnt, ROW_ALIGN)
    seg = jnp.cumsum(pc, axis=1) - pc
    region = _round_up(jnp.sum(pc, axis=0), tmx)
    region_end = jnp.cumsum(region)
    dst = (region_end - region)[None, :] + jnp.cumsum(pc, axis=0) - pc
    n_x_tiles = (TOP_K * n + n_tok_tiles * n_experts * (ROW_ALIGN - 1)) // tmx + n_experts
    n_active = (region_end[-1:] // tmx).astype(I32)
    x_tile_start = jnp.arange(n_x_tiles, dtype=I32) * tmx
    tile_expert = jnp.sum((x_tile_start[:, None] >= region_end[None, :]).astype(I32), axis=1)
    tile_expert = jnp.minimum(tile_expert, tile_expert[n_active[0] - 1]).astype(I32)
    eid = meta[0:TOP_K].astype(I32)
    tok_tile = jnp.arange(n, dtype=I32) // tm
    slot = seg[tok_tile[None, :], eid] + meta[4:4 + TOP_K].astype(I32)
    slots = jnp.zeros((8, n), I32).at[:TOP_K].set(slot)
    pc_f, seg_f, dst_f = (t.reshape(-1).astype(I32) for t in (pc, seg, dst))

    xs = _scatter_rows(hn, slots, pc_f, seg_f, dst_f, n_x_tiles * tmx, tm, n_experts)
    ys = _expert_ffn(xs, tile_expert, n_active, l1_moe_w_gate_up.astype(BF16),
                     l1_moe_w_down.astype(BF16), tmx)
    out = _combine(h, meta, slots, pc_f, seg_f, dst_f, ys, tm, n_experts)
    return out.reshape(batch, seq, d).astype(x.dtype)
```

```python
import functools
import math

import jax
import jax.numpy as jnp
from jax import lax
from jax.experimental import pallas as pl
from jax.experimental.pallas import tpu as pltpu

F32 = jnp.float32
BF16 = jnp.bfloat16
I32 = jnp.int32

NORM_EPS = 1e-6
LANE = 128
HGRN_CHUNK = 64
DIFF_HEAD_DIM = 64
TOP_K = 2
LOG2E = math.log2(math.e)
EXP_CLAMP = 60.0
VMEM_LIMIT = 56 * 1024 * 1024

TILES = dict(
    tm_in=512,
    tb_rec=512,
    tm_proj=512,
    tm_ffn=512,
    tq=512,
    tm_router=512,
    tmx=1024,
    fbx=512,
)


def _cparams(sem):
    return pltpu.CompilerParams(dimension_semantics=sem, vmem_limit_bytes=VMEM_LIMIT)


def _sigmoid(x):
    return 1.0 / (1.0 + jnp.exp(-x))


def _silu(x):
    return x * _sigmoid(x)


def _rms_rows(x, gain):
    ms = jnp.mean(x * x, axis=-1, keepdims=True)
    return x * lax.rsqrt(ms + NORM_EPS) * gain


def _dot(a, b):
    return jnp.dot(a, b, preferred_element_type=F32)


def _dot_nt(a, b):
    return lax.dot_general(a, b, (((1,), (1,)), ((), ())), preferred_element_type=F32)


def _dot_tn(a, b):
    return lax.dot_general(a, b, (((0,), (0,)), ((), ())), preferred_element_type=F32)


def _hgrn_in_kernel(x_ref, g_ref, lb_ref, w_ref, q_ref, k_ref, lf_ref, v_ref, gs_ref):
    d = x_ref.shape[1]
    xn = _rms_rows(x_ref[...], g_ref[...]).astype(BF16)

    def proj(j):
        return _dot(xn, w_ref[:, j * d:(j + 1) * d])

    q_ref[...] = _silu(proj(0)).astype(BF16)
    lb = lb_ref[...]
    forget = lb + (1.0 - lb) * _sigmoid(proj(1))
    k_ref[...] = (1.0 - forget).astype(BF16)
    lf_ref[...] = jnp.log(forget)
    v_ref[...] = proj(2).astype(BF16)
    gs_ref[...] = _silu(proj(3)).astype(BF16)


def _hgrn_in(x, gain, lb, w_bf16):
    n, d = x.shape
    tm = min(TILES["tm_in"], n)
    row = pl.BlockSpec((tm, d), lambda i: (i, 0))
    vec = pl.BlockSpec((1, d), lambda i: (0, 0))
    return pl.pallas_call(
        _hgrn_in_kernel,
        grid=(n // tm,),
        in_specs=[row, vec, vec, pl.BlockSpec((d, 4 * d), lambda i: (0, 0))],
        out_specs=[row, row, row, row, row],
        out_shape=[
            jax.ShapeDtypeStruct((n, d), BF16),
            jax.ShapeDtypeStruct((n, d), BF16),
            jax.ShapeDtypeStruct((n, d), F32),
            jax.ShapeDtypeStruct((n, d), BF16),
            jax.ShapeDtypeStruct((n, d), BF16),
        ],
        compiler_params=_cparams(("arbitrary",)),
        name="hgrn_in",
    )(x, gain, lb, w_bf16)


def _hgrn_rec_kernel(q_ref, k_ref, lf_ref, v_ref, gs_ref, on_ref, o_ref, st_ref, *, chunk, heads):
    @pl.when(pl.program_id(1) == 0)
    def _():
        st_ref[...] = jnp.zeros_like(st_ref)

    c = chunk
    n_chunks = q_ref.shape[0] // c
    row = lax.broadcasted_iota(I32, (c, c), 0)
    col = lax.broadcasted_iota(I32, (c, c), 1)
    causal = col <= row
    tri = causal.astype(BF16)
    mid = c // 2 - 1

    def body(ci, carry):
        r0 = pl.multiple_of(ci * c, c)
        rows = pl.ds(r0, c)
        lf = lf_ref[rows, :]
        hi = lf.astype(BF16)
        lo = (lf - hi.astype(F32)).astype(BF16)
        b = _dot(tri, hi) + _dot(tri, lo)
        bmid = b[mid:mid + 1, :]
        bend = b[c - 1:c, :]
        e_q = jnp.exp(jnp.minimum(b - bmid, EXP_CLAMP))
        e_k = jnp.exp(jnp.minimum(bmid - b, EXP_CLAMP))
        qt = (q_ref[rows, :].astype(F32) * e_q).astype(BF16)
        kt = (k_ref[rows, :].astype(F32) * e_k).astype(BF16)
        v = v_ref[rows, :]
        gs = gs_ref[rows, :].astype(F32)
        e_mid = jnp.exp(bmid)
        e_end = jnp.exp(bend)
        e_end_mid = jnp.exp(bend - bmid)
        for h in range(heads):
            sl = slice(h * LANE, (h + 1) * LANE)
            qh, kh, vh = qt[:, sl], kt[:, sl], v[:, sl]
            a = jnp.where(causal, _dot_nt(qh, kh), 0.0).astype(BF16)
            st = st_ref[h]
            o = _dot(a, vh) + _dot_nt(qh, (st * e_mid[:, sl]).astype(BF16))
            st_ref[h] = st * e_end[:, sl] + _dot_tn(vh, kh) * e_end_mid[:, sl]
            og = _rms_rows(o, on_ref[:, sl]) * gs[:, sl]
            o_ref[rows, sl] = og.astype(BF16)
        return carry

    lax.fori_loop(0, n_chunks, body, 0, unroll=True)


def _hgrn_rec(q, k, lf, v, gs, out_norm, batch, seq):
    n, d = q.shape
    heads = d // LANE
    tb = min(TILES["tb_rec"], seq)
    nt = seq // tb
    blk = pl.BlockSpec((tb, d), lambda b, t: (b * nt + t, 0))
    return pl.pallas_call(
        functools.partial(_hgrn_rec_kernel, chunk=HGRN_CHUNK, heads=heads),
        grid=(batch, nt),
        in_specs=[blk, blk, blk, blk, blk, pl.BlockSpec((1, d), lambda b, t: (0, 0))],
        out_specs=blk,
        out_shape=jax.ShapeDtypeStruct((n, d), BF16),
        scratch_shapes=[pltpu.VMEM((heads, LANE, LANE), F32)],
        compiler_params=_cparams(("arbitrary", "arbitrary")),
        name="hgrn_rec",
    )(q, k, lf, v, gs, out_norm)


def _proj_res_kernel(a_ref, w_ref, r_ref, o_ref):
    o_ref[...] = r_ref[...] + _dot(a_ref[...], w_ref[...])


def _proj_res(a, w_bf16, res):
    n, d = res.shape
    tm = min(TILES["tm_proj"], n)
    row = pl.BlockSpec((tm, d), lambda i: (i, 0))
    return pl.pallas_call(
        _proj_res_kernel,
        grid=(n // tm,),
        in_specs=[row, pl.BlockSpec((d, d), lambda i: (0, 0)), row],
        out_specs=row,
        out_shape=jax.ShapeDtypeStruct((n, d), F32),
        compiler_params=_cparams(("arbitrary",)),
        name="proj_res",
    )(a, w_bf16, res)


def _ffn_kernel(h_ref, g_ref, wgu_ref, wd_ref, o_ref):
    f = wd_ref.shape[0]
    h = h_ref[...]
    xn = _rms_rows(h, g_ref[...]).astype(BF16)
    gu = _dot(xn, wgu_ref[...])
    act = (_silu(gu[:, :f]) * gu[:, f:]).astype(BF16)
    o_ref[...] = h + _dot(act, wd_ref[...])


def _hidden_block(f, want):
    best = None
    for fb in range(LANE, f + 1, LANE):
        if f % fb == 0 and fb <= want:
            best = fb
    assert best is not None, f
    return best


def _ffn(h, gain, w_gate_up_bf16, w_down_bf16):
    n, d = h.shape
    f = w_down_bf16.shape[0]
    tm = min(TILES["tm_ffn"], n)
    row = pl.BlockSpec((tm, d), lambda i: (i, 0))
    once = pl.Buffered(1)
    return pl.pallas_call(
        _ffn_kernel,
        grid=(n // tm,),
        in_specs=[
            row,
            pl.BlockSpec((1, d), lambda i: (0, 0)),
            pl.BlockSpec((d, 2 * f), lambda i: (0, 0), pipeline_mode=once),
            pl.BlockSpec((f, d), lambda i: (0, 0), pipeline_mode=once),
        ],
        out_specs=row,
        out_shape=jax.ShapeDtypeStruct((n, d), F32),
        compiler_params=_cparams(("arbitrary",)),
        name="ffn",
    )(h, gain, w_gate_up_bf16, w_down_bf16)


def _group_rms(y, gmat, gain):
    d = y.shape[1]
    sq = (y * y).astype(BF16)
    ms = jnp.concatenate(
        [_dot(sq[:, s * LANE:(s + 1) * LANE], gmat) for s in range(d // LANE)], axis=1)
    return y * lax.rsqrt(ms + NORM_EPS) * gain


def _diff_in_kernel(x_ref, g_ref, w_ref, gm_ref, qn_ref, kn_ref, q_ref, k_ref, v_ref):
    d = x_ref.shape[1]
    xn = _rms_rows(x_ref[...], g_ref[...]).astype(BF16)

    def proj(j):
        return _dot(xn, w_ref[:, j * d:(j + 1) * d])

    q_ref[...] = (_group_rms(proj(0), gm_ref[...], qn_ref[...]) * (DIFF_HEAD_DIM ** -0.5 * LOG2E)).astype(BF16)
    k_ref[...] = _group_rms(proj(1), gm_ref[...], kn_ref[...]).astype(BF16)
    v_ref[...] = proj(2).astype(BF16)


def _diff_in(x, gain, w_bf16, q_norm_row, k_norm_row):
    n, d = x.shape
    tm = min(TILES["tm_in"], n)
    lane_group = jnp.arange(LANE) // DIFF_HEAD_DIM
    gmat = ((lane_group[:, None] == lane_group[None, :]).astype(F32) / DIFF_HEAD_DIM).astype(BF16)
    row = pl.BlockSpec((tm, d), lambda i: (i, 0))
    vec = pl.BlockSpec((1, d), lambda i: (0, 0))
    return pl.pallas_call(
        _diff_in_kernel,
        grid=(n // tm,),
        in_specs=[row, vec, pl.BlockSpec((d, 3 * d), lambda i: (0, 0)),
                  pl.BlockSpec((LANE, LANE), lambda i: (0, 0)), vec, vec],
        out_specs=[row, row, row],
        out_shape=[jax.ShapeDtypeStruct((n, d), BF16)] * 3,
        compiler_params=_cparams(("arbitrary",)),
        name="diff_in",
    )(x, gain, w_bf16, gmat, q_norm_row, k_norm_row)


V_ROWS = LANE + 16
ATTN_UNROLL = 4


def _attn_kernel(slope_ref, q_ref, k_ref, vt_ref, lam_ref, sn_ref, o_ref, m_ref, acc_ref, w_ref,
                 *, tq, lam_init):
    h = pl.program_id(1)
    qb = pl.program_id(2)
    slope = slope_ref[h] * LOG2E

    q = q_ref[...]
    lane = lax.broadcasted_iota(I32, q.shape, 1)
    zero = jnp.zeros_like(q)
    qs = jnp.concatenate([jnp.where(lane < DIFF_HEAD_DIM, q, zero),
                          jnp.where(lane >= DIFF_HEAD_DIM, q, zero)], axis=0)

    m_ref[...] = jnp.full_like(m_ref, -jnp.inf)
    acc_ref[...] = jnp.zeros_like(acc_ref)

    lane_k = lax.broadcasted_iota(I32, (tq, LANE), 1)
    bias = slope * lax.broadcasted_iota(I32, (tq, LANE), 0).astype(F32)
    bias_hi = bias.astype(BF16).astype(F32)
    bias_cols = jnp.where(lane_k == 0, bias_hi, jnp.where(lane_k == 1, bias - bias_hi, 0.0)).astype(BF16)
    lane_q = lax.broadcasted_iota(I32, (2 * tq, LANE), 1)
    qsx = jnp.concatenate([qs, jnp.where(lane_q < 2, 1.0, 0.0).astype(BF16)], axis=1)

    def scores(kb):
        k0 = pl.multiple_of(kb * tq, tq)
        return _dot_nt(jnp.concatenate([k_ref[pl.ds(k0, tq), :], bias_cols], axis=1), qsx)

    def consume(kb, w, masked):
        if masked:
            r = lax.broadcasted_iota(I32, w.shape, 0)
            c = lax.broadcasted_iota(I32, w.shape, 1)
            w = jnp.where(r <= jnp.where(c >= tq, c - tq, c), w, -jnp.inf)
        off = slope * ((kb - qb) * tq).astype(F32)
        m_old = m_ref[...]
        m_new = jnp.maximum(m_old, jnp.max(w, axis=0, keepdims=True) + off)
        alpha = jnp.exp2(m_old - m_new)
        p = jnp.exp2(w - (m_new - off)).astype(BF16)
        k0 = pl.multiple_of(kb * tq, tq)
        acc_ref[...] = alpha * acc_ref[...] + _dot(vt_ref[:, pl.ds(k0, tq)], p)
        m_ref[...] = m_new

    w_ref[0] = scores(0)

    def run(first, count, last_masked):
        for u in range(count):
            if u + 1 < count or not last_masked:
                w_ref[(u + 1) % 2] = scores(first + u + 1)
            consume(first + u, w_ref[u % 2], last_masked and u + 1 == count)

    def group(g, carry):
        run(ATTN_UNROLL * g, ATTN_UNROLL, False)
        return carry

    n_groups = qb // ATTN_UNROLL
    lax.fori_loop(0, n_groups, group, 0)
    rest = qb - ATTN_UNROLL * n_groups
    for r in range(ATTN_UNROLL):
        @pl.when(rest == r)
        def _(r=r):
            run(qb - r, r + 1, True)

    lam_rows = lam_ref[...]
    lam = (jnp.exp(jnp.sum(lam_rows[0:1] * lam_rows[1:2], axis=1, keepdims=True))
           - jnp.exp(jnp.sum(lam_rows[2:3] * lam_rows[3:4], axis=1, keepdims=True)) + lam_init)
    acc = acc_ref[...]
    on = acc[:LANE] / acc[LANE:LANE + 1]
    o = (on[:, :tq] - lam * on[:, tq:]).T
    o_ref[...] = (_rms_rows(o, sn_ref[...]) * (1.0 - lam_init)).astype(BF16)


def _attention(q, k, v, slopes, lam_rows, sub_norm_row, batch, seq, lam_init):
    n, d = q.shape
    heads = d // LANE
    tq = min(TILES["tq"], seq)
    nq = seq // tq
    vt = v.reshape(batch, seq, heads, LANE).transpose(0, 2, 3, 1)
    vt = jnp.concatenate([vt, jnp.ones((batch, heads, V_ROWS - LANE, seq), BF16)], axis=2)
    grid_spec = pltpu.PrefetchScalarGridSpec(
        num_scalar_prefetch=1,
        grid=(batch, heads, nq),
        in_specs=[
            pl.BlockSpec((tq, LANE), lambda b, h, i, s: (b * nq + i, h)),
            pl.BlockSpec((seq, LANE), lambda b, h, i, s: (b, h)),
            pl.BlockSpec((None, None, V_ROWS, seq), lambda b, h, i, s: (b, h, 0, 0)),
            pl.BlockSpec((8, LANE), lambda b, h, i, s: (0, 0)),
            pl.BlockSpec((1, LANE), lambda b, h, i, s: (0, 0)),
        ],
        out_specs=pl.BlockSpec((tq, LANE), lambda b, h, i, s: (b * nq + i, h)),
        scratch_shapes=[pltpu.VMEM((1, 2 * tq), F32), pltpu.VMEM((V_ROWS, 2 * tq), F32),
                        pltpu.VMEM((2, tq, 2 * tq), F32)],
    )
    return pl.pallas_call(
        functools.partial(_attn_kernel, tq=tq, lam_init=lam_init),
        grid_spec=grid_spec,
        out_shape=jax.ShapeDtypeStruct((n, d), BF16),
        compiler_params=_cparams(("arbitrary", "arbitrary", "arbitrary")),
        name="diff_attn",
    )(slopes, q, k, vt, lam_rows, sub_norm_row)


ROUTE_ROWS = 16
ROW_ALIGN = 8


def _proj_router_kernel(a_ref, w_ref, r_ref, g_ref, rw_ref, triu_ref, h_ref, hn_ref, meta_ref, cnt_ref,
                        *, n_experts):
    h = r_ref[...] + _dot(a_ref[...], w_ref[...])
    h_ref[...] = h
    hn = _rms_rows(h, g_ref[...])
    hn_hi = hn.astype(BF16)
    hn_ref[...] = hn_hi
    hn_lo = (hn - hn_hi.astype(F32)).astype(BF16)
    both = _dot(hn_hi, rw_ref[...])
    logits = both[:, :LANE] + both[:, LANE:] + _dot(hn_lo, rw_ref[:, :LANE])
    lt = logits.T[:ROUTE_ROWS]
    sub = lax.broadcasted_iota(I32, lt.shape, 0)
    subf = sub.astype(F32)
    neg = jnp.float32(-jnp.inf)
    lt = jnp.where(sub < n_experts, lt, neg)
    v1 = jnp.max(lt, axis=0, keepdims=True)
    i1 = jnp.min(jnp.where(lt == v1, subf, float(ROUTE_ROWS)), axis=0, keepdims=True)
    oh1 = subf == i1
    rest = jnp.where(oh1, neg, lt)
    v2 = jnp.max(rest, axis=0, keepdims=True)
    i2 = jnp.min(jnp.where(rest == v2, subf, float(ROUTE_ROWS)), axis=0, keepdims=True)
    oh2 = subf == i2
    e = jnp.exp(v2 - v1)
    w1 = 1.0 / (1.0 + e)
    w2 = e * w1
    cnt = jnp.where(oh1, 1.0, 0.0) + jnp.where(oh2, 1.0, 0.0)
    before = _dot(cnt.astype(BF16), triu_ref[...])
    counts = jnp.sum(cnt, axis=1, keepdims=True) + jnp.zeros((ROUTE_ROWS, LANE), F32)
    padded = jnp.floor((counts + (ROW_ALIGN - 1)) * (1.0 / ROW_ALIGN)) * ROW_ALIGN
    er = lax.broadcasted_iota(I32, (ROUTE_ROWS, ROUTE_ROWS), 0)
    ec = lax.broadcasted_iota(I32, (ROUTE_ROWS, ROUTE_ROWS), 1)
    seg = _dot(jnp.where(ec < er, 1.0, 0.0).astype(BF16), padded.astype(BF16))[:, 0:1]
    s1 = jnp.sum(jnp.where(oh1, before + seg, 0.0), axis=0, keepdims=True)
    s2 = jnp.sum(jnp.where(oh2, before + seg, 0.0), axis=0, keepdims=True)
    meta = jnp.where(sub == 0, i1, jnp.where(sub == 1, i2, jnp.where(sub == 2, w1, jnp.where(
        sub == 3, w2, jnp.where(sub == 4, s1, jnp.where(sub == 5, s2, 0.0))))))
    meta_ref[...] = meta[:8]
    cnt_ref[...] = counts


def _proj_router(a, w_bf16, res, gain, router):
    n, d = res.shape
    n_experts = router.shape[1]
    assert n_experts <= ROUTE_ROWS
    tm = min(TILES["tm_router"], n)
    rw = jnp.zeros((d, LANE), F32).at[:, :n_experts].set(router.astype(F32))
    rw_hi = rw.astype(BF16)
    rw_cat = jnp.concatenate([rw_hi, (rw - rw_hi.astype(F32)).astype(BF16)], axis=1)
    triu = jnp.triu(jnp.ones((tm, tm), F32), 1).astype(BF16)
    row = pl.BlockSpec((tm, d), lambda i: (i, 0))
    return pl.pallas_call(
        functools.partial(_proj_router_kernel, n_experts=n_experts),
        grid=(n // tm,),
        in_specs=[row, pl.BlockSpec((d, d), lambda i: (0, 0)), row,
                  pl.BlockSpec((1, d), lambda i: (0, 0)),
                  pl.BlockSpec((d, 2 * LANE), lambda i: (0, 0)),
                  pl.BlockSpec((tm, tm), lambda i: (0, 0))],
        out_specs=[row, row, pl.BlockSpec((8, tm), lambda i: (0, i)),
                   pl.BlockSpec((ROUTE_ROWS, LANE), lambda i: (i, 0))],
        out_shape=[jax.ShapeDtypeStruct((n, d), F32), jax.ShapeDtypeStruct((n, d), BF16),
                   jax.ShapeDtypeStruct((8, n), F32),
                   jax.ShapeDtypeStruct((n // tm * ROUTE_ROWS, LANE), F32)],
        compiler_params=_cparams(("arbitrary",)),
        name="proj_router",
    )(a, w_bf16, res, gain, rw_cat, triu)


def _segment_copies(pc_ref, seg_ref, dst_ref, tile, n_experts, max_rows, make_copy, start):
    sizes = []
    size = ROW_ALIGN
    while size <= max_rows:
        sizes.append(size)
        size *= 2
    for e in range(n_experts):
        idx = tile * n_experts + e
        n_rows = pc_ref[idx]
        local0 = seg_ref[idx]
        sorted0 = dst_ref[idx]
        done = 0
        for size in reversed(sizes):
            take = (n_rows & size) != 0
            local = pl.multiple_of(local0 + done, ROW_ALIGN)
            glob = pl.multiple_of(sorted0 + done, ROW_ALIGN)

            @pl.when(take)
            def _(local=local, glob=glob, size=size):
                cp = make_copy(local, glob, size)
                if start:
                    cp.start()
                else:
                    cp.wait()

            done = done + jnp.where(take, size, 0)


def _slot_one_hot(meta_ref, n_slots, gated):
    tokens = meta_ref.shape[1]
    sub = lax.broadcasted_iota(I32, (n_slots, tokens), 0)
    out = None
    for kk in range(TOP_K):
        slot = meta_ref[4 + kk:5 + kk, :].astype(I32)
        val = meta_ref[2 + kk:3 + kk, :] if gated else 1.0
        term = jnp.where(sub == slot, val, 0.0)
        out = term if out is None else out + term
    return out.astype(BF16)


def _scatter_kernel(pc_ref, seg_ref, dst_ref, meta_ref, hn_ref, xs_in_ref, xs_ref, sorted_ref, sem,
                    *, n_experts, n_slots):
    del xs_in_ref
    tile = pl.program_id(0)
    sorted_ref[...] = _dot(_slot_one_hot(meta_ref, n_slots, False), hn_ref[...])

    def make_copy(local, glob, size):
        return pltpu.make_async_copy(sorted_ref.at[pl.ds(local, size)], xs_ref.at[pl.ds(glob, size)], sem)

    max_rows = hn_ref.shape[0]
    _segment_copies(pc_ref, seg_ref, dst_ref, tile, n_experts, max_rows, make_copy, True)
    _segment_copies(pc_ref, seg_ref, dst_ref, tile, n_experts, max_rows, make_copy, False)


def _scatter_rows(hn, meta, pc, seg, dst, n_sorted, tm, n_experts):
    n, d = hn.shape
    n_slots = TOP_K * tm + ROW_ALIGN * n_experts
    grid_spec = pltpu.PrefetchScalarGridSpec(
        num_scalar_prefetch=3,
        grid=(n // tm,),
        in_specs=[pl.BlockSpec((8, tm), lambda i, *_: (0, i)),
                  pl.BlockSpec((tm, d), lambda i, *_: (i, 0)),
                  pl.BlockSpec(memory_space=pl.ANY)],
        out_specs=pl.BlockSpec(memory_space=pl.ANY),
        scratch_shapes=[pltpu.VMEM((n_slots, d), F32), pltpu.SemaphoreType.DMA(())],
    )
    return pl.pallas_call(
        functools.partial(_scatter_kernel, n_experts=n_experts, n_slots=n_slots),
        grid_spec=grid_spec,
        out_shape=jax.ShapeDtypeStruct((n_sorted, d), F32),
        input_output_aliases={5: 0},
        compiler_params=pltpu.CompilerParams(dimension_semantics=("arbitrary",),
                                             vmem_limit_bytes=VMEM_LIMIT, has_side_effects=True),
        name="moe_scatter",
    )(pc, seg, dst, meta, hn, jnp.zeros((n_sorted, d), F32))


EXPERT_SUB_ROWS = 512


def _expert_ffn_kernel(te_ref, na_ref, x_ref, wg_ref, wu_ref, wd_ref, y_ref, xb_ref, acc_ref):
    del te_ref
    i = pl.program_id(0)
    j = pl.program_id(1)

    @pl.when(i < na_ref[0])
    def _():
        @pl.when(j == 0)
        def _():
            xb_ref[...] = x_ref[...].astype(BF16)
            acc_ref[...] = jnp.zeros_like(acc_ref)

        sub_rows = min(EXPERT_SUB_ROWS, xb_ref.shape[0])
        for r0 in range(0, xb_ref.shape[0], sub_rows):
            rows = slice(r0, r0 + sub_rows)
            xb = xb_ref[rows, :]
            act = (_silu(_dot(xb, wg_ref[...])) * _dot(xb, wu_ref[...])).astype(BF16)
            acc_ref[rows, :] += _dot(act, wd_ref[...])

        @pl.when(j == pl.num_programs(1) - 1)
        def _():
            y_ref[...] = acc_ref[...]

    @pl.when((i >= na_ref[0]) & (j == pl.num_programs(1) - 1))
    def _():
        y_ref[...] = jnp.zeros_like(y_ref)


def _expert_ffn(xs, tile_expert, n_active, w_gate_up_bf16, w_down_bf16, tmx):
    n_sorted, d = xs.shape
    f = w_down_bf16.shape[1]
    fb = _hidden_block(f, TILES["fbx"])
    nf = f // fb
    n_tiles = n_sorted // tmx

    def row_map(i, j, te, na):
        return (jnp.minimum(i, na[0] - 1), 0)

    def col(i, j, na):
        return jnp.where(i < na[0], j, nf - 1)

    grid_spec = pltpu.PrefetchScalarGridSpec(
        num_scalar_prefetch=2,
        grid=(n_tiles, nf),
        in_specs=[
            pl.BlockSpec((tmx, d), row_map),
            pl.BlockSpec((None, d, fb), lambda i, j, te, na: (te[i], 0, col(i, j, na))),
            pl.BlockSpec((None, d, fb), lambda i, j, te, na: (te[i], 0, col(i, j, na) + nf)),
            pl.BlockSpec((None, fb, d), lambda i, j, te, na: (te[i], col(i, j, na), 0)),
        ],
        out_specs=pl.BlockSpec((tmx, d), lambda i, j, te, na: (i, 0)),
        scratch_shapes=[pltpu.VMEM((tmx, d), BF16), pltpu.VMEM((tmx, d), F32)],
    )
    return pl.pallas_call(
        _expert_ffn_kernel,
        grid_spec=grid_spec,
        out_shape=jax.ShapeDtypeStruct((n_sorted, d), F32),
        compiler_params=_cparams(("arbitrary", "arbitrary")),
        name="expert_ffn",
    )(tile_expert, n_active, xs, w_gate_up_bf16, w_gate_up_bf16, w_down_bf16)


def _combine_kernel(pc_ref, seg_ref, dst_ref, meta_ref, h_ref, ys_ref, o_ref, buf_ref, sem,
                    *, n_experts, n_slots):
    tile = pl.program_id(0)

    @pl.when(tile == 0)
    def _():
        buf_ref[...] = jnp.zeros_like(buf_ref)

    def make_copy(local, glob, size):
        return pltpu.make_async_copy(ys_ref.at[pl.ds(glob, size)], buf_ref.at[pl.ds(local, size)], sem)

    max_rows = h_ref.shape[0]
    _segment_copies(pc_ref, seg_ref, dst_ref, tile, n_experts, max_rows, make_copy, True)
    gates = _slot_one_hot(meta_ref, n_slots, True)
    _segment_copies(pc_ref, seg_ref, dst_ref, tile, n_experts, max_rows, make_copy, False)
    o_ref[...] = h_ref[...] + _dot_tn(gates, buf_ref[...].astype(BF16))


def _combine(h, meta, pc, seg, dst, ys, tm, n_experts):
    n, d = h.shape
    n_slots = TOP_K * tm + ROW_ALIGN * n_experts
    grid_spec = pltpu.PrefetchScalarGridSpec(
        num_scalar_prefetch=3,
        grid=(n // tm,),
        in_specs=[pl.BlockSpec((8, tm), lambda i, *_: (0, i)),
                  pl.BlockSpec((tm, d), lambda i, *_: (i, 0)),
                  pl.BlockSpec(memory_space=pl.ANY)],
        out_specs=pl.BlockSpec((tm, d), lambda i, *_: (i, 0)),
        scratch_shapes=[pltpu.VMEM((n_slots, d), F32), pltpu.SemaphoreType.DMA(())],
    )
    return pl.pallas_call(
        functools.partial(_combine_kernel, n_experts=n_experts, n_slots=n_slots),
        grid_spec=grid_spec,
        out_shape=jax.ShapeDtypeStruct((n, d), F32),
        compiler_params=_cparams(("arbitrary",)),
        name="moe_combine",
    )(pc, seg, dst, meta, h, ys)


def _row(vec):
    return vec.astype(F32).reshape(1, -1)


def _round_up(x, m):
    return (x + m - 1) // m * m


def kernel(x, lower_bounds, l0_mix_norm, l0_hgrn_w_in, l0_hgrn_out_norm, l0_hgrn_w_out, l0_ffn_norm, l0_ffn_w_gate_up, l0_ffn_w_down, l1_mix_norm, l1_diff_w_in, l1_q_norm, l1_k_norm, l1_lambda_q1, l1_lambda_k1, l1_lambda_q2, l1_lambda_k2, l1_diff_sub_norm, l1_diff_w_out, l1_ffn_norm, l1_router, l1_moe_w_gate_up, l1_moe_w_down):
    batch, seq, d = x.shape
    n = batch * seq
    heads = d // LANE
    n_experts = l1_router.shape[1]
    x2 = x.reshape(n, d).astype(F32)

    lb0 = jnp.cumsum(jax.nn.softmax(lower_bounds.astype(F32), axis=0), axis=0)[0]
    q, k, lf, v, gs = _hgrn_in(x2, _row(l0_mix_norm), _row(lb0), l0_hgrn_w_in.astype(BF16))
    og = _hgrn_rec(q, k, lf, v, gs, _row(l0_hgrn_out_norm), batch, seq)
    h = _proj_res(og, l0_hgrn_w_out.astype(BF16), x2)
    h = _ffn(h, _row(l0_ffn_norm), l0_ffn_w_gate_up.astype(BF16), l0_ffn_w_down.astype(BF16))

    lam_init = 0.8 - 0.6 * math.exp(-0.3 * 1)
    q, k, v = _diff_in(h, _row(l1_mix_norm), l1_diff_w_in.astype(BF16),
                       _row(jnp.tile(l1_q_norm, 2 * heads)), _row(jnp.tile(l1_k_norm, 2 * heads)))
    slopes = 2.0 ** (-8.0 * jnp.arange(1, heads + 1, dtype=F32) / heads)
    lam_rows = jnp.zeros((8, LANE), F32).at[:4, :DIFF_HEAD_DIM].set(
        jnp.stack([l1_lambda_q1, l1_lambda_k1, l1_lambda_q2, l1_lambda_k2]).astype(F32))
    oa = _attention(q, k, v, slopes, lam_rows, _row(l1_diff_sub_norm), batch, seq, lam_init)

    h, hn, meta, tile_cnt = _proj_router(oa, l1_diff_w_out.astype(BF16), h, _row(l1_ffn_norm), l1_router)
    tm = min(TILES["tm_router"], n)
    n_tok_tiles = n // tm
    tmx = min(TILES["tmx"], n)
    cnt = tile_cnt.reshape(n_tok_tiles, ROUTE_ROWS, LANE)[:, :n_experts, 0].astype(I32)
    pc = _round_up(cnt, ROW_ALIGN)
    seg = jnp.cumsum(pc, axis=1) - pc
    region = _round_up(jnp.sum(pc, axis=0), tmx)
    region_end = jnp.cumsum(region)
    dst = (region_end - region)[None, :] + jnp.cumsum(pc, axis=0) - pc
    n_x_tiles = (TOP_K * n + n_tok_tiles * n_experts * (ROW_ALIGN - 1)) // tmx + n_experts
    n_active = (region_end[-1:] // tmx).astype(I32)
    x_tile_start = jnp.arange(n_x_tiles, dtype=I32) * tmx
    tile_expert = jnp.sum((x_tile_start[:, None] >= region_end[None, :]).astype(I32), axis=1)
    tile_expert = jnp.minimum(tile_expert, tile_expert[n_active[0] - 1]).astype(I32)
    pc_f, seg_f, dst_f = (t.reshape(-1).astype(I32) for t in (pc, seg, dst))

    xs = _scatter_rows(hn, meta, pc_f, seg_f, dst_f, n_x_tiles * tmx, tm, n_experts)
    ys = _expert_ffn(xs, tile_expert, n_active, l1_moe_w_gate_up.astype(BF16),
                     l1_moe_w_down.astype(BF16), tmx)
    out = _combine(h, meta, pc_f, seg_f, dst_f, ys, tm, n_experts)
    return out.reshape(batch, seq, d).astype(x.dtype)
```

```python
import functools
import math

import jax
import jax.numpy as jnp
from jax import lax
from jax.experimental import pallas as pl
from jax.experimental.pallas import tpu as pltpu

F32 = jnp.float32
BF16 = jnp.bfloat16
I32 = jnp.int32

NORM_EPS = 1e-6
LANE = 128
HGRN_CHUNK = 64
DIFF_HEAD_DIM = 64
TOP_K = 2
LOG2E = math.log2(math.e)
EXP_CLAMP = 60.0
VMEM_LIMIT = 56 * 1024 * 1024

TILES = dict(
    tm_in=512,
    tb_rec=512,
    tm_proj=512,
    tm_ffn=512,
    tq=1024,
    tm_router=512,
    tmx=1024,
    fbx=512,
)


def _cparams(sem):
    return pltpu.CompilerParams(dimension_semantics=sem, vmem_limit_bytes=VMEM_LIMIT)


def _sigmoid(x):
    return 1.0 / (1.0 + jnp.exp(-x))


def _silu(x):
    return x * _sigmoid(x)


def _rms_rows(x, gain):
    ms = jnp.mean(x * x, axis=-1, keepdims=True)
    return x * lax.rsqrt(ms + NORM_EPS) * gain


def _dot(a, b):
    return jnp.dot(a, b, preferred_element_type=F32)


def _dot_nt(a, b):
    return lax.dot_general(a, b, (((1,), (1,)), ((), ())), preferred_element_type=F32)


def _dot_tn(a, b):
    return lax.dot_general(a, b, (((0,), (0,)), ((), ())), preferred_element_type=F32)


def _hgrn_in_kernel(x_ref, g_ref, lb_ref, w_ref, q_ref, k_ref, lf_ref, v_ref, gs_ref):
    d = x_ref.shape[1]
    xn = _rms_rows(x_ref[...], g_ref[...]).astype(BF16)

    def proj(j):
        return _dot(xn, w_ref[:, j * d:(j + 1) * d])

    q_ref[...] = _silu(proj(0)).astype(BF16)
    lb = lb_ref[...]
    forget = lb + (1.0 - lb) * _sigmoid(proj(1))
    k_ref[...] = (1.0 - forget).astype(BF16)
    lf_ref[...] = jnp.log(forget)
    v_ref[...] = proj(2).astype(BF16)
    gs_ref[...] = _silu(proj(3)).astype(BF16)


def _hgrn_in(x, gain, lb, w_bf16):
    n, d = x.shape
    tm = min(TILES["tm_in"], n)
    row = pl.BlockSpec((tm, d), lambda i: (i, 0))
    vec = pl.BlockSpec((1, d), lambda i: (0, 0))
    return pl.pallas_call(
        _hgrn_in_kernel,
        grid=(n // tm,),
        in_specs=[row, vec, vec, pl.BlockSpec((d, 4 * d), lambda i: (0, 0))],
        out_specs=[row, row, row, row, row],
        out_shape=[
            jax.ShapeDtypeStruct((n, d), BF16),
            jax.ShapeDtypeStruct((n, d), BF16),
            jax.ShapeDtypeStruct((n, d), F32),
            jax.ShapeDtypeStruct((n, d), BF16),
            jax.ShapeDtypeStruct((n, d), BF16),
        ],
        compiler_params=_cparams(("arbitrary",)),
        name="hgrn_in",
    )(x, gain, lb, w_bf16)


def _hgrn_rec_kernel(q_ref, k_ref, lf_ref, v_ref, gs_ref, on_ref, o_ref, st_ref, *, chunk, heads):
    @pl.when(pl.program_id(1) == 0)
    def _():
        st_ref[...] = jnp.zeros_like(st_ref)

    c = chunk
    n_chunks = q_ref.shape[0] // c
    row = lax.broadcasted_iota(I32, (c, c), 0)
    col = lax.broadcasted_iota(I32, (c, c), 1)
    causal = col <= row
    tri = causal.astype(BF16)
    mid = c // 2 - 1

    def body(ci, carry):
        r0 = pl.multiple_of(ci * c, c)
        rows = pl.ds(r0, c)
        lf = lf_ref[rows, :]
        hi = lf.astype(BF16)
        lo = (lf - hi.astype(F32)).astype(BF16)
        b = _dot(tri, hi) + _dot(tri, lo)
        bmid = b[mid:mid + 1, :]
        bend = b[c - 1:c, :]
        e_q = jnp.exp(jnp.minimum(b - bmid, EXP_CLAMP))
        e_k = jnp.exp(jnp.minimum(bmid - b, EXP_CLAMP))
        qt = (q_ref[rows, :].astype(F32) * e_q).astype(BF16)
        kt = (k_ref[rows, :].astype(F32) * e_k).astype(BF16)
        v = v_ref[rows, :]
        gs = gs_ref[rows, :].astype(F32)
        e_mid = jnp.exp(bmid)
        e_end = jnp.exp(bend)
        e_end_mid = jnp.exp(bend - bmid)
        for h in range(heads):
            sl = slice(h * LANE, (h + 1) * LANE)
            qh, kh, vh = qt[:, sl], kt[:, sl], v[:, sl]
            a = jnp.where(causal, _dot_nt(qh, kh), 0.0).astype(BF16)
            st = st_ref[h]
            o = _dot(a, vh) + _dot_nt(qh, (st * e_mid[:, sl]).astype(BF16))
            st_ref[h] = st * e_end[:, sl] + _dot_tn(vh, kh) * e_end_mid[:, sl]
            og = _rms_rows(o, on_ref[:, sl]) * gs[:, sl]
            o_ref[rows, sl] = og.astype(BF16)
        return carry

    lax.fori_loop(0, n_chunks, body, 0, unroll=True)


def _hgrn_rec(q, k, lf, v, gs, out_norm, batch, seq):
    n, d = q.shape
    heads = d // LANE
    tb = min(TILES["tb_rec"], seq)
    nt = seq // tb
    blk = pl.BlockSpec((tb, d), lambda b, t: (b * nt + t, 0))
    return pl.pallas_call(
        functools.partial(_hgrn_rec_kernel, chunk=HGRN_CHUNK, heads=heads),
        grid=(batch, nt),
        in_specs=[blk, blk, blk, blk, blk, pl.BlockSpec((1, d), lambda b, t: (0, 0))],
        out_specs=blk,
        out_shape=jax.ShapeDtypeStruct((n, d), BF16),
        scratch_shapes=[pltpu.VMEM((heads, LANE, LANE), F32)],
        compiler_params=_cparams(("arbitrary", "arbitrary")),
        name="hgrn_rec",
    )(q, k, lf, v, gs, out_norm)


def _proj_res_kernel(a_ref, w_ref, r_ref, o_ref):
    o_ref[...] = r_ref[...] + _dot(a_ref[...], w_ref[...])


def _proj_res(a, w_bf16, res):
    n, d = res.shape
    tm = min(TILES["tm_proj"], n)
    row = pl.BlockSpec((tm, d), lambda i: (i, 0))
    return pl.pallas_call(
        _proj_res_kernel,
        grid=(n // tm,),
        in_specs=[row, pl.BlockSpec((d, d), lambda i: (0, 0)), row],
        out_specs=row,
        out_shape=jax.ShapeDtypeStruct((n, d), F32),
        compiler_params=_cparams(("arbitrary",)),
        name="proj_res",
    )(a, w_bf16, res)


def _ffn_kernel(h_ref, g_ref, wgu_ref, wd_ref, o_ref):
    f = wd_ref.shape[0]
    h = h_ref[...]
    xn = _rms_rows(h, g_ref[...]).astype(BF16)
    gu = _dot(xn, wgu_ref[...])
    act = (_silu(gu[:, :f]) * gu[:, f:]).astype(BF16)
    o_ref[...] = h + _dot(act, wd_ref[...])


def _hidden_block(f, want):
    best = None
    for fb in range(LANE, f + 1, LANE):
        if f % fb == 0 and fb <= want:
            best = fb
    assert best is not None, f
    return best


def _ffn(h, gain, w_gate_up_bf16, w_down_bf16):
    n, d = h.shape
    f = w_down_bf16.shape[0]
    tm = min(TILES["tm_ffn"], n)
    row = pl.BlockSpec((tm, d), lambda i: (i, 0))
    once = pl.Buffered(1)
    return pl.pallas_call(
        _ffn_kernel,
        grid=(n // tm,),
        in_specs=[
            row,
            pl.BlockSpec((1, d), lambda i: (0, 0)),
            pl.BlockSpec((d, 2 * f), lambda i: (0, 0), pipeline_mode=once),
            pl.BlockSpec((f, d), lambda i: (0, 0), pipeline_mode=once),
        ],
        out_specs=row,
        out_shape=jax.ShapeDtypeStruct((n, d), F32),
        compiler_params=_cparams(("arbitrary",)),
        name="ffn",
    )(h, gain, w_gate_up_bf16, w_down_bf16)


def _group_rms(y, gmat, gain):
    d = y.shape[1]
    sq = (y * y).astype(BF16)
    ms = jnp.concatenate(
        [_dot(sq[:, s * LANE:(s + 1) * LANE], gmat) for s in range(d // LANE)], axis=1)
    return y * lax.rsqrt(ms + NORM_EPS) * gain


V_ROWS = LANE + 16


def _diff_in_kernel(x_ref, g_ref, w_ref, gm_ref, qn_ref, kn_ref, q_ref, k_ref, vt_ref):
    d = x_ref.shape[1]
    xn = _rms_rows(x_ref[...], g_ref[...]).astype(BF16)

    def proj(j):
        return _dot(xn, w_ref[:, j * d:(j + 1) * d])

    q_ref[...] = (_group_rms(proj(0), gm_ref[...], qn_ref[...]) * (DIFF_HEAD_DIM ** -0.5 * LOG2E)).astype(BF16)
    k_ref[...] = _group_rms(proj(1), gm_ref[...], kn_ref[...]).astype(BF16)
    v = proj(2)
    for h in range(d // LANE):
        vt_ref[h, :LANE, :] = v[:, h * LANE:(h + 1) * LANE].T.astype(BF16)
        vt_ref[h, LANE:, :] = jnp.ones((V_ROWS - LANE, v.shape[0]), BF16)


def _diff_in(x, gain, w_bf16, q_norm_row, k_norm_row, batch, seq):
    n, d = x.shape
    heads = d // LANE
    tm = min(TILES["tm_in"], seq)
    nt = seq // tm
    lane_group = jnp.arange(LANE) // DIFF_HEAD_DIM
    gmat = ((lane_group[:, None] == lane_group[None, :]).astype(F32) / DIFF_HEAD_DIM).astype(BF16)
    row = pl.BlockSpec((tm, d), lambda i: (i, 0))
    vec = pl.BlockSpec((1, d), lambda i: (0, 0))
    return pl.pallas_call(
        _diff_in_kernel,
        grid=(n // tm,),
        in_specs=[row, vec, pl.BlockSpec((d, 3 * d), lambda i: (0, 0)),
                  pl.BlockSpec((LANE, LANE), lambda i: (0, 0)), vec, vec],
        out_specs=[row, row, pl.BlockSpec((None, heads, V_ROWS, tm), lambda i: (i // nt, 0, 0, i % nt))],
        out_shape=[jax.ShapeDtypeStruct((n, d), BF16), jax.ShapeDtypeStruct((n, d), BF16),
                   jax.ShapeDtypeStruct((batch, heads, V_ROWS, seq), BF16)],
        compiler_params=_cparams(("arbitrary",)),
        name="diff_in",
    )(x, gain, w_bf16, gmat, q_norm_row, k_norm_row)


ATTN_UNROLL = 4


def _attn_kernel(slope_ref, q_ref, k_ref, vt_ref, lam_ref, sn_ref, o_ref, m_ref, acc_ref, w_ref,
                 *, tq, lam_init):
    h = pl.program_id(1)
    qb = pl.program_id(2)
    slope = slope_ref[h] * LOG2E

    q = q_ref[...]
    lane = lax.broadcasted_iota(I32, q.shape, 1)
    zero = jnp.zeros_like(q)
    qs = jnp.concatenate([jnp.where(lane < DIFF_HEAD_DIM, q, zero),
                          jnp.where(lane >= DIFF_HEAD_DIM, q, zero)], axis=0)

    m_ref[...] = jnp.full_like(m_ref, -jnp.inf)
    acc_ref[...] = jnp.zeros_like(acc_ref)

    lane_k = lax.broadcasted_iota(I32, (tq, LANE), 1)
    bias = slope * lax.broadcasted_iota(I32, (tq, LANE), 0).astype(F32)
    bias_hi = bias.astype(BF16).astype(F32)
    bias_cols = jnp.where(lane_k == 0, bias_hi, jnp.where(lane_k == 1, bias - bias_hi, 0.0)).astype(BF16)
    lane_q = lax.broadcasted_iota(I32, (2 * tq, LANE), 1)
    qsx = jnp.concatenate([qs, jnp.where(lane_q < 2, 1.0, 0.0).astype(BF16)], axis=1)

    def scores(kb):
        k0 = pl.multiple_of(kb * tq, tq)
        return _dot_nt(jnp.concatenate([k_ref[pl.ds(k0, tq), :], bias_cols], axis=1), qsx)

    def consume(kb, w, masked):
        if masked:
            r = lax.broadcasted_iota(I32, w.shape, 0)
            c = lax.broadcasted_iota(I32, w.shape, 1)
            w = jnp.where(r <= jnp.where(c >= tq, c - tq, c), w, -jnp.inf)
        off = slope * ((kb - qb) * tq).astype(F32)
        m_old = m_ref[...]
        m_new = jnp.maximum(m_old, jnp.max(w, axis=0, keepdims=True) + off)
        alpha = jnp.exp2(m_old - m_new)
        p = jnp.exp2(w - (m_new - off)).astype(BF16)
        k0 = pl.multiple_of(kb * tq, tq)
        acc_ref[...] = alpha * acc_ref[...] + _dot(vt_ref[:, pl.ds(k0, tq)], p)
        m_ref[...] = m_new

    w_ref[0] = scores(0)

    def run(first, count, last_masked):
        for u in range(count):
            if u + 1 < count or not last_masked:
                w_ref[(u + 1) % 2] = scores(first + u + 1)
            consume(first + u, w_ref[u % 2], last_masked and u + 1 == count)

    def group(g, carry):
        run(ATTN_UNROLL * g, ATTN_UNROLL, False)
        return carry

    n_groups = qb // ATTN_UNROLL
    lax.fori_loop(0, n_groups, group, 0)
    rest = qb - ATTN_UNROLL * n_groups
    for r in range(ATTN_UNROLL):
        @pl.when(rest == r)
        def _(r=r):
            run(qb - r, r + 1, True)

    lam_rows = lam_ref[...]
    lam = (jnp.exp(jnp.sum(lam_rows[0:1] * lam_rows[1:2], axis=1, keepdims=True))
           - jnp.exp(jnp.sum(lam_rows[2:3] * lam_rows[3:4], axis=1, keepdims=True)) + lam_init)
    acc = acc_ref[...]
    on = acc[:LANE] / acc[LANE:LANE + 1]
    o = (on[:, :tq] - lam * on[:, tq:]).T
    o_ref[...] = (_rms_rows(o, sn_ref[...]) * (1.0 - lam_init)).astype(BF16)


def _attention(q, k, vt, slopes, lam_rows, sub_norm_row, batch, seq, lam_init):
    n, d = q.shape
    heads = d // LANE
    tq = min(TILES["tq"], seq)
    nq = seq // tq
    grid_spec = pltpu.PrefetchScalarGridSpec(
        num_scalar_prefetch=1,
        grid=(batch, heads, nq),
        in_specs=[
            pl.BlockSpec((tq, LANE), lambda b, h, i, s: (b * nq + i, h)),
            pl.BlockSpec((seq, LANE), lambda b, h, i, s: (b, h)),
            pl.BlockSpec((None, None, V_ROWS, seq), lambda b, h, i, s: (b, h, 0, 0)),
            pl.BlockSpec((8, LANE), lambda b, h, i, s: (0, 0)),
            pl.BlockSpec((1, LANE), lambda b, h, i, s: (0, 0)),
        ],
        out_specs=pl.BlockSpec((tq, LANE), lambda b, h, i, s: (b * nq + i, h)),
        scratch_shapes=[pltpu.VMEM((1, 2 * tq), F32), pltpu.VMEM((V_ROWS, 2 * tq), F32),
                        pltpu.VMEM((2, tq, 2 * tq), F32)],
    )
    return pl.pallas_call(
        functools.partial(_attn_kernel, tq=tq, lam_init=lam_init),
        grid_spec=grid_spec,
        out_shape=jax.ShapeDtypeStruct((n, d), BF16),
        compiler_params=_cparams(("arbitrary", "arbitrary", "arbitrary")),
        name="diff_attn",
    )(slopes, q, k, vt, lam_rows, sub_norm_row)


ROUTE_ROWS = 16
ROW_ALIGN = 8


def _proj_router_kernel(a_ref, w_ref, r_ref, g_ref, rw_ref, triu_ref, h_ref, hn_ref, meta_ref, cnt_ref,
                        *, n_experts):
    h = r_ref[...] + _dot(a_ref[...], w_ref[...])
    h_ref[...] = h
    hn = _rms_rows(h, g_ref[...])
    hn_hi = hn.astype(BF16)
    hn_ref[...] = hn_hi
    hn_lo = (hn - hn_hi.astype(F32)).astype(BF16)
    both = _dot(hn_hi, rw_ref[...])
    logits = both[:, :LANE] + both[:, LANE:] + _dot(hn_lo, rw_ref[:, :LANE])
    lt = logits.T[:ROUTE_ROWS]
    sub = lax.broadcasted_iota(I32, lt.shape, 0)
    subf = sub.astype(F32)
    neg = jnp.float32(-jnp.inf)
    lt = jnp.where(sub < n_experts, lt, neg)
    v1 = jnp.max(lt, axis=0, keepdims=True)
    i1 = jnp.min(jnp.where(lt == v1, subf, float(ROUTE_ROWS)), axis=0, keepdims=True)
    oh1 = subf == i1
    rest = jnp.where(oh1, neg, lt)
    v2 = jnp.max(rest, axis=0, keepdims=True)
    i2 = jnp.min(jnp.where(rest == v2, subf, float(ROUTE_ROWS)), axis=0, keepdims=True)
    oh2 = subf == i2
    e = jnp.exp(v2 - v1)
    w1 = 1.0 / (1.0 + e)
    w2 = e * w1
    cnt = jnp.where(oh1, 1.0, 0.0) + jnp.where(oh2, 1.0, 0.0)
    before = _dot(cnt.astype(BF16), triu_ref[...])
    counts = jnp.sum(cnt, axis=1, keepdims=True) + jnp.zeros((ROUTE_ROWS, LANE), F32)
    padded = jnp.floor((counts + (ROW_ALIGN - 1)) * (1.0 / ROW_ALIGN)) * ROW_ALIGN
    er = lax.broadcasted_iota(I32, (ROUTE_ROWS, ROUTE_ROWS), 0)
    ec = lax.broadcasted_iota(I32, (ROUTE_ROWS, ROUTE_ROWS), 1)
    seg = _dot(jnp.where(ec < er, 1.0, 0.0).astype(BF16), padded.astype(BF16))[:, 0:1]
    s1 = jnp.sum(jnp.where(oh1, before + seg, 0.0), axis=0, keepdims=True)
    s2 = jnp.sum(jnp.where(oh2, before + seg, 0.0), axis=0, keepdims=True)
    meta = jnp.where(sub == 0, i1, jnp.where(sub == 1, i2, jnp.where(sub == 2, w1, jnp.where(
        sub == 3, w2, jnp.where(sub == 4, s1, jnp.where(sub == 5, s2, 0.0))))))
    meta_ref[...] = meta[:8]
    cnt_ref[...] = counts


def _proj_router(a, w_bf16, res, gain, router):
    n, d = res.shape
    n_experts = router.shape[1]
    assert n_experts <= ROUTE_ROWS
    tm = min(TILES["tm_router"], n)
    rw = jnp.zeros((d, LANE), F32).at[:, :n_experts].set(router.astype(F32))
    rw_hi = rw.astype(BF16)
    rw_cat = jnp.concatenate([rw_hi, (rw - rw_hi.astype(F32)).astype(BF16)], axis=1)
    triu = jnp.triu(jnp.ones((tm, tm), F32), 1).astype(BF16)
    row = pl.BlockSpec((tm, d), lambda i: (i, 0))
    return pl.pallas_call(
        functools.partial(_proj_router_kernel, n_experts=n_experts),
        grid=(n // tm,),
        in_specs=[row, pl.BlockSpec((d, d), lambda i: (0, 0)), row,
                  pl.BlockSpec((1, d), lambda i: (0, 0)),
                  pl.BlockSpec((d, 2 * LANE), lambda i: (0, 0)),
                  pl.BlockSpec((tm, tm), lambda i: (0, 0))],
        out_specs=[row, row, pl.BlockSpec((8, tm), lambda i: (0, i)),
                   pl.BlockSpec((ROUTE_ROWS, LANE), lambda i: (i, 0))],
        out_shape=[jax.ShapeDtypeStruct((n, d), F32), jax.ShapeDtypeStruct((n, d), BF16),
                   jax.ShapeDtypeStruct((8, n), F32),
                   jax.ShapeDtypeStruct((n // tm * ROUTE_ROWS, LANE), F32)],
        compiler_params=_cparams(("arbitrary",)),
        name="proj_router",
    )(a, w_bf16, res, gain, rw_cat, triu)


def _for_each_piece(n_rows, max_rows, fn):
    size = ROW_ALIGN
    while size * 2 <= max_rows:
        size *= 2
    done = 0
    while size >= ROW_ALIGN:
        take = (n_rows & size) != 0

        @pl.when(take)
        def _(done=done, size=size):
            fn(done, size)

        done = done + jnp.where(take, size, 0)
        size //= 2


def _segment_copies(pc_ref, seg_ref, dst_ref, tile, n_experts, max_rows, make_copy, start):
    for e in range(n_experts):
        idx = tile * n_experts + e
        local0 = seg_ref[idx]
        sorted0 = dst_ref[idx]

        def piece(off, size, local0=local0, sorted0=sorted0):
            cp = make_copy(pl.multiple_of(local0 + off, ROW_ALIGN), pl.multiple_of(sorted0 + off, ROW_ALIGN), size)
            if start:
                cp.start()
            else:
                cp.wait()

        _for_each_piece(pc_ref[idx], max_rows, piece)


def _slot_one_hot(meta_ref, n_slots, gated):
    tokens = meta_ref.shape[1]
    sub = lax.broadcasted_iota(I32, (n_slots, tokens), 0)
    out = None
    for kk in range(TOP_K):
        slot = meta_ref[4 + kk:5 + kk, :].astype(I32)
        val = meta_ref[2 + kk:3 + kk, :] if gated else 1.0
        term = jnp.where(sub == slot, val, 0.0)
        out = term if out is None else out + term
    return out.astype(BF16)


ZERO_ROWS = 512


def _scatter_kernel(pc_ref, seg_ref, dst_ref, gap_ref, meta_ref, hn_ref, xs_ref, sorted_ref, zeros_ref, sem, zsem,
                    *, n_experts, n_slots, tmx, max_empty_tiles):
    tile = pl.program_id(0)
    last = pl.num_programs(0) - 1
    max_rows = hn_ref.shape[0]

    def copies(t, b, start):
        def make_copy(local, glob, size):
            return pltpu.make_async_copy(sorted_ref.at[b, pl.ds(local, size)], xs_ref.at[pl.ds(glob, size)],
                                         sem.at[b])
        _segment_copies(pc_ref, seg_ref, dst_ref, t, n_experts, max_rows, make_copy, start)

    def zero_fill(start):
        def go(row0, size):
            cp = pltpu.make_async_copy(zeros_ref.at[pl.ds(0, size)],
                                       xs_ref.at[pl.ds(pl.multiple_of(row0, ROW_ALIGN), size)], zsem)
            if start:
                cp.start()
            else:
                cp.wait()

        zrows = zeros_ref.shape[0]
        for e in range(n_experts):
            row0 = gap_ref[2 * e]
            _for_each_piece(gap_ref[2 * e + 1], zrows, lambda off, size, row0=row0: go(row0 + off, size))
        for t in range(max_empty_tiles):
            @pl.when(t < gap_ref[2 * n_experts + 1])
            def _(t=t):
                for r0 in range(0, tmx, zrows):
                    go(gap_ref[2 * n_experts] + t * tmx + r0, zrows)

    @pl.when(tile == 0)
    def _():
        zeros_ref[...] = jnp.zeros_like(zeros_ref)
        zero_fill(True)
        zero_fill(False)

    one_hot = _slot_one_hot(meta_ref, n_slots, False)
    for b in range(2):
        @pl.when(tile % 2 == b)
        def _(b=b):
            sorted_ref[b] = _dot(one_hot, hn_ref[...])
            copies(tile, b, True)

            @pl.when(tile > 0)
            def _():
                copies(tile - 1, 1 - b, False)

            @pl.when(tile == last)
            def _():
                copies(tile, b, False)


def _scatter_rows(hn, meta, pc, seg, dst, gaps, n_sorted, tm, tmx, n_experts):
    n, d = hn.shape
    n_slots = TOP_K * tm + ROW_ALIGN * n_experts
    assert tmx % ZERO_ROWS == 0 or tmx < ZERO_ROWS
    grid_spec = pltpu.PrefetchScalarGridSpec(
        num_scalar_prefetch=4,
        grid=(n // tm,),
        in_specs=[pl.BlockSpec((8, tm), lambda i, *_: (0, i)),
                  pl.BlockSpec((tm, d), lambda i, *_: (i, 0))],
        out_specs=pl.BlockSpec(memory_space=pl.ANY),
        scratch_shapes=[pltpu.VMEM((2, n_slots, d), F32), pltpu.VMEM((min(ZERO_ROWS, tmx), d), F32),
                        pltpu.SemaphoreType.DMA((2,)), pltpu.SemaphoreType.DMA(())],
    )
    return pl.pallas_call(
        functools.partial(_scatter_kernel, n_experts=n_experts, n_slots=n_slots, tmx=tmx,
                          max_empty_tiles=n_sorted // tmx - (TOP_K * n) // tmx),
        grid_spec=grid_spec,
        out_shape=jax.ShapeDtypeStruct((n_sorted, d), F32),
        compiler_params=pltpu.CompilerParams(dimension_semantics=("arbitrary",),
                                             vmem_limit_bytes=VMEM_LIMIT, has_side_effects=True),
        name="moe_scatter",
    )(pc, seg, dst, gaps, meta, hn)


EXPERT_SUB_ROWS = 512


def _expert_ffn_kernel(te_ref, na_ref, x_ref, wg_ref, wu_ref, wd_ref, y_ref, xb_ref, acc_ref):
    del te_ref
    i = pl.program_id(0)
    j = pl.program_id(1)

    @pl.when(i < na_ref[0])
    def _():
        @pl.when(j == 0)
        def _():
            xb_ref[...] = x_ref[...].astype(BF16)
            acc_ref[...] = jnp.zeros_like(acc_ref)

        sub_rows = min(EXPERT_SUB_ROWS, xb_ref.shape[0])
        for r0 in range(0, xb_ref.shape[0], sub_rows):
            rows = slice(r0, r0 + sub_rows)
            xb = xb_ref[rows, :]
            act = (_silu(_dot(xb, wg_ref[...])) * _dot(xb, wu_ref[...])).astype(BF16)
            acc_ref[rows, :] += _dot(act, wd_ref[...])

        @pl.when(j == pl.num_programs(1) - 1)
        def _():
            y_ref[...] = acc_ref[...]

    @pl.when((i >= na_ref[0]) & (j == pl.num_programs(1) - 1))
    def _():
        y_ref[...] = jnp.zeros_like(y_ref)


def _expert_ffn(xs, tile_expert, n_active, w_gate_up_bf16, w_down_bf16, tmx):
    n_sorted, d = xs.shape
    f = w_down_bf16.shape[1]
    fb = _hidden_block(f, TILES["fbx"])
    nf = f // fb
    n_tiles = n_sorted // tmx

    def row_map(i, j, te, na):
        return (jnp.minimum(i, na[0] - 1), 0)

    def col(i, j, na):
        return jnp.where(i < na[0], j, nf - 1)

    grid_spec = pltpu.PrefetchScalarGridSpec(
        num_scalar_prefetch=2,
        grid=(n_tiles, nf),
        in_specs=[
            pl.BlockSpec((tmx, d), row_map),
            pl.BlockSpec((None, d, fb), lambda i, j, te, na: (te[i], 0, col(i, j, na))),
            pl.BlockSpec((None, d, fb), lambda i, j, te, na: (te[i], 0, col(i, j, na) + nf)),
            pl.BlockSpec((None, fb, d), lambda i, j, te, na: (te[i], col(i, j, na), 0)),
        ],
        out_specs=pl.BlockSpec((tmx, d), lambda i, j, te, na: (i, 0)),
        scratch_shapes=[pltpu.VMEM((tmx, d), BF16), pltpu.VMEM((tmx, d), F32)],
    )
    return pl.pallas_call(
        _expert_ffn_kernel,
        grid_spec=grid_spec,
        out_shape=jax.ShapeDtypeStruct((n_sorted, d), F32),
        compiler_params=_cparams(("arbitrary", "arbitrary")),
        name="expert_ffn",
    )(tile_expert, n_active, xs, w_gate_up_bf16, w_gate_up_bf16, w_down_bf16)


def _combine_kernel(pc_ref, seg_ref, dst_ref, meta_ref, h_ref, ys_ref, o_ref, buf_ref, sem,
                    *, n_experts, n_slots):
    tile = pl.program_id(0)
    last = pl.num_programs(0) - 1
    max_rows = h_ref.shape[0]

    def copies(t, b, start):
        def make_copy(local, glob, size):
            return pltpu.make_async_copy(ys_ref.at[pl.ds(glob, size)], buf_ref.at[b, pl.ds(local, size)],
                                         sem.at[b])
        _segment_copies(pc_ref, seg_ref, dst_ref, t, n_experts, max_rows, make_copy, start)

    @pl.when(tile == 0)
    def _():
        buf_ref[...] = jnp.zeros_like(buf_ref)
        copies(0, 0, True)

    gates = _slot_one_hot(meta_ref, n_slots, True)
    for b in range(2):
        @pl.when(tile % 2 == b)
        def _(b=b):
            @pl.when(tile < last)
            def _():
                copies(tile + 1, 1 - b, True)

            copies(tile, b, False)
            o_ref[...] = h_ref[...] + _dot_tn(gates, buf_ref[b].astype(BF16))


def _combine(h, meta, pc, seg, dst, ys, tm, n_experts):
    n, d = h.shape
    n_slots = TOP_K * tm + ROW_ALIGN * n_experts
    grid_spec = pltpu.PrefetchScalarGridSpec(
        num_scalar_prefetch=3,
        grid=(n // tm,),
        in_specs=[pl.BlockSpec((8, tm), lambda i, *_: (0, i)),
                  pl.BlockSpec((tm, d), lambda i, *_: (i, 0)),
                  pl.BlockSpec(memory_space=pl.ANY)],
        out_specs=pl.BlockSpec((tm, d), lambda i, *_: (i, 0)),
        scratch_shapes=[pltpu.VMEM((2, n_slots, d), F32), pltpu.SemaphoreType.DMA((2,))],
    )
    return pl.pallas_call(
        functools.partial(_combine_kernel, n_experts=n_experts, n_slots=n_slots),
        grid_spec=grid_spec,
        out_shape=jax.ShapeDtypeStruct((n, d), F32),
        compiler_params=_cparams(("arbitrary",)),
        name="moe_combine",
    )(pc, seg, dst, meta, h, ys)


def _row(vec):
    return vec.astype(F32).reshape(1, -1)


def _round_up(x, m):
    return (x + m - 1) // m * m


def kernel(x, lower_bounds, l0_mix_norm, l0_hgrn_w_in, l0_hgrn_out_norm, l0_hgrn_w_out, l0_ffn_norm, l0_ffn_w_gate_up, l0_ffn_w_down, l1_mix_norm, l1_diff_w_in, l1_q_norm, l1_k_norm, l1_lambda_q1, l1_lambda_k1, l1_lambda_q2, l1_lambda_k2, l1_diff_sub_norm, l1_diff_w_out, l1_ffn_norm, l1_router, l1_moe_w_gate_up, l1_moe_w_down):
    batch, seq, d = x.shape
    n = batch * seq
    heads = d // LANE
    n_experts = l1_router.shape[1]
    x2 = x.reshape(n, d).astype(F32)

    lb0 = jnp.cumsum(jax.nn.softmax(lower_bounds.astype(F32), axis=0), axis=0)[0]
    q, k, lf, v, gs = _hgrn_in(x2, _row(l0_mix_norm), _row(lb0), l0_hgrn_w_in.astype(BF16))
    og = _hgrn_rec(q, k, lf, v, gs, _row(l0_hgrn_out_norm), batch, seq)
    h = _proj_res(og, l0_hgrn_w_out.astype(BF16), x2)
    h = _ffn(h, _row(l0_ffn_norm), l0_ffn_w_gate_up.astype(BF16), l0_ffn_w_down.astype(BF16))

    lam_init = 0.8 - 0.6 * math.exp(-0.3 * 1)
    q, k, vt = _diff_in(h, _row(l1_mix_norm), l1_diff_w_in.astype(BF16),
                        _row(jnp.tile(l1_q_norm, 2 * heads)), _row(jnp.tile(l1_k_norm, 2 * heads)), batch, seq)
    slopes = 2.0 ** (-8.0 * jnp.arange(1, heads + 1, dtype=F32) / heads)
    lam_rows = jnp.zeros((8, LANE), F32).at[:4, :DIFF_HEAD_DIM].set(
        jnp.stack([l1_lambda_q1, l1_lambda_k1, l1_lambda_q2, l1_lambda_k2]).astype(F32))
    oa = _attention(q, k, vt, slopes, lam_rows, _row(l1_diff_sub_norm), batch, seq, lam_init)

    h, hn, meta, tile_cnt = _proj_router(oa, l1_diff_w_out.astype(BF16), h, _row(l1_ffn_norm), l1_router)
    tm = min(TILES["tm_router"], n)
    n_tok_tiles = n // tm
    tmx = min(TILES["tmx"], n)
    cnt = tile_cnt.reshape(n_tok_tiles, ROUTE_ROWS, LANE)[:, :n_experts, 0].astype(I32)
    pc = _round_up(cnt, ROW_ALIGN)
    seg = jnp.cumsum(pc, axis=1) - pc
    region = _round_up(jnp.sum(pc, axis=0), tmx)
    region_end = jnp.cumsum(region)
    dst = (region_end - region)[None, :] + jnp.cumsum(pc, axis=0) - pc
    n_x_tiles = (TOP_K * n + n_tok_tiles * n_experts * (ROW_ALIGN - 1)) // tmx + n_experts
    n_active = (region_end[-1:] // tmx).astype(I32)
    x_tile_start = jnp.arange(n_x_tiles, dtype=I32) * tmx
    tile_expert = jnp.sum((x_tile_start[:, None] >= region_end[None, :]).astype(I32), axis=1)
    tile_expert = jnp.minimum(tile_expert, tile_expert[n_active[0] - 1]).astype(I32)
    pc_f, seg_f, dst_f = (t.reshape(-1).astype(I32) for t in (pc, seg, dst))
    used = jnp.sum(pc, axis=0)
    gaps = jnp.concatenate([jnp.stack([region_end - region + used, region - used], axis=1).reshape(-1),
                            region_end[-1:], n_x_tiles - n_active]).astype(I32)

    xs = _scatter_rows(hn, meta, pc_f, seg_f, dst_f, gaps, n_x_tiles * tmx, tm, tmx, n_experts)
    ys = _expert_ffn(xs, tile_expert, n_active, l1_moe_w_gate_up.astype(BF16),
                     l1_moe_w_down.astype(BF16), tmx)
    out = _combine(h, meta, pc_f, seg_f, dst_f, ys, tm, n_experts)
    return out.reshape(batch, seq, d).astype(x.dtype)
```

```python
import functools
import math

import jax
import jax.numpy as jnp
from jax import lax
from jax.experimental import pallas as pl
from jax.experimental.pallas import tpu as pltpu

F32 = jnp.float32
BF16 = jnp.bfloat16
I32 = jnp.int32

NORM_EPS = 1e-6
LANE = 128
HGRN_CHUNK = 64
DIFF_HEAD_DIM = 64
TOP_K = 2
LOG2E = math.log2(math.e)
EXP_CLAMP = 60.0
VMEM_LIMIT = 56 * 1024 * 1024

TILES = dict(
    tm_in=512,
    tb_rec=512,
    tm_ffn=512,
    tq=1024,
    tm_router=512,
    tmx=1024,
    fbx=512,
)


def _cparams(sem):
    return pltpu.CompilerParams(dimension_semantics=sem, vmem_limit_bytes=VMEM_LIMIT)


def _sigmoid(x):
    return 1.0 / (1.0 + jnp.exp(-x))


def _silu(x):
    return x * _sigmoid(x)


def _rms_rows(x, gain):
    ms = jnp.mean(x * x, axis=-1, keepdims=True)
    return x * lax.rsqrt(ms + NORM_EPS) * gain


def _dot(a, b):
    return jnp.dot(a, b, preferred_element_type=F32)


def _dot_nt(a, b):
    return lax.dot_general(a, b, (((1,), (1,)), ((), ())), preferred_element_type=F32)


def _dot_tn(a, b):
    return lax.dot_general(a, b, (((0,), (0,)), ((), ())), preferred_element_type=F32)


def _hgrn_in_kernel(x_ref, g_ref, lb_ref, w_ref, q_ref, k_ref, lf_ref, v_ref, gs_ref):
    d = x_ref.shape[1]
    xn = _rms_rows(x_ref[...], g_ref[...]).astype(BF16)

    def proj(j):
        return _dot(xn, w_ref[:, j * d:(j + 1) * d])

    q_ref[...] = _silu(proj(0)).astype(BF16)
    lb = lb_ref[...]
    forget = lb + (1.0 - lb) * _sigmoid(proj(1))
    k_ref[...] = (1.0 - forget).astype(BF16)
    lf_ref[...] = jnp.log(forget)
    v_ref[...] = proj(2).astype(BF16)
    gs_ref[...] = _silu(proj(3)).astype(BF16)


def _hgrn_in(x, gain, lb, w_bf16):
    n, d = x.shape
    tm = min(TILES["tm_in"], n)
    row = pl.BlockSpec((tm, d), lambda i: (i, 0))
    vec = pl.BlockSpec((1, d), lambda i: (0, 0))
    return pl.pallas_call(
        _hgrn_in_kernel,
        grid=(n // tm,),
        in_specs=[row, vec, vec, pl.BlockSpec((d, 4 * d), lambda i: (0, 0))],
        out_specs=[row, row, row, row, row],
        out_shape=[
            jax.ShapeDtypeStruct((n, d), BF16),
            jax.ShapeDtypeStruct((n, d), BF16),
            jax.ShapeDtypeStruct((n, d), F32),
            jax.ShapeDtypeStruct((n, d), BF16),
            jax.ShapeDtypeStruct((n, d), BF16),
        ],
        compiler_params=_cparams(("arbitrary",)),
        name="hgrn_in",
    )(x, gain, lb, w_bf16)


def _hgrn_rec_kernel(q_ref, k_ref, lf_ref, v_ref, gs_ref, on_ref, o_ref, st_ref, *, chunk, heads):
    @pl.when(pl.program_id(1) == 0)
    def _():
        st_ref[...] = jnp.zeros_like(st_ref)

    c = chunk
    n_chunks = q_ref.shape[0] // c
    row = lax.broadcasted_iota(I32, (c, c), 0)
    col = lax.broadcasted_iota(I32, (c, c), 1)
    causal = col <= row
    tri = causal.astype(BF16)
    mid = c // 2 - 1

    def body(ci, carry):
        r0 = pl.multiple_of(ci * c, c)
        rows = pl.ds(r0, c)
        lf = lf_ref[rows, :]
        hi = lf.astype(BF16)
        lo = (lf - hi.astype(F32)).astype(BF16)
        b = _dot(tri, hi) + _dot(tri, lo)
        bmid = b[mid:mid + 1, :]
        bend = b[c - 1:c, :]
        e_q = jnp.exp(jnp.minimum(b - bmid, EXP_CLAMP))
        e_k = jnp.exp(jnp.minimum(bmid - b, EXP_CLAMP))
        qt = (q_ref[rows, :].astype(F32) * e_q).astype(BF16)
        kt = (k_ref[rows, :].astype(F32) * e_k).astype(BF16)
        v = v_ref[rows, :]
        gs = gs_ref[rows, :].astype(F32)
        e_mid = jnp.exp(bmid)
        e_end = jnp.exp(bend)
        e_end_mid = jnp.exp(bend - bmid)
        for h in range(heads):
            sl = slice(h * LANE, (h + 1) * LANE)
            qh, kh, vh = qt[:, sl], kt[:, sl], v[:, sl]
            a = jnp.where(causal, _dot_nt(qh, kh), 0.0).astype(BF16)
            st = st_ref[h]
            o = _dot(a, vh) + _dot_nt(qh, (st * e_mid[:, sl]).astype(BF16))
            st_ref[h] = st * e_end[:, sl] + _dot_tn(vh, kh) * e_end_mid[:, sl]
            og = _rms_rows(o, on_ref[:, sl]) * gs[:, sl]
            o_ref[rows, sl] = og.astype(BF16)
        return carry

    lax.fori_loop(0, n_chunks, body, 0, unroll=True)


def _hgrn_rec(q, k, lf, v, gs, out_norm, batch, seq):
    n, d = q.shape
    heads = d // LANE
    tb = min(TILES["tb_rec"], seq)
    nt = seq // tb
    blk = pl.BlockSpec((tb, d), lambda b, t: (b * nt + t, 0))
    return pl.pallas_call(
        functools.partial(_hgrn_rec_kernel, chunk=HGRN_CHUNK, heads=heads),
        grid=(batch, nt),
        in_specs=[blk, blk, blk, blk, blk, pl.BlockSpec((1, d), lambda b, t: (0, 0))],
        out_specs=blk,
        out_shape=jax.ShapeDtypeStruct((n, d), BF16),
        scratch_shapes=[pltpu.VMEM((heads, LANE, LANE), F32)],
        compiler_params=_cparams(("arbitrary", "arbitrary")),
        name="hgrn_rec",
    )(q, k, lf, v, gs, out_norm)


def _ffn_kernel(a_ref, wo_ref, x_ref, g_ref, wgu_ref, wd_ref, o_ref):
    f = wd_ref.shape[0]
    h = x_ref[...] + _dot(a_ref[...], wo_ref[...])
    xn = _rms_rows(h, g_ref[...]).astype(BF16)
    gu = _dot(xn, wgu_ref[...])
    act = (_silu(gu[:, :f]) * gu[:, f:]).astype(BF16)
    o_ref[...] = h + _dot(act, wd_ref[...])


def _hidden_block(f, want):
    best = None
    for fb in range(LANE, f + 1, LANE):
        if f % fb == 0 and fb <= want:
            best = fb
    assert best is not None, f
    return best


def _ffn(mix, w_out_bf16, x, gain, w_gate_up_bf16, w_down_bf16):
    n, d = x.shape
    f = w_down_bf16.shape[0]
    tm = min(TILES["tm_ffn"], n)
    row = pl.BlockSpec((tm, d), lambda i: (i, 0))
    once = pl.Buffered(1)
    return pl.pallas_call(
        _ffn_kernel,
        grid=(n // tm,),
        in_specs=[
            row,
            pl.BlockSpec((d, d), lambda i: (0, 0), pipeline_mode=once),
            row,
            pl.BlockSpec((1, d), lambda i: (0, 0)),
            pl.BlockSpec((d, 2 * f), lambda i: (0, 0), pipeline_mode=once),
            pl.BlockSpec((f, d), lambda i: (0, 0), pipeline_mode=once),
        ],
        out_specs=row,
        out_shape=jax.ShapeDtypeStruct((n, d), F32),
        compiler_params=_cparams(("arbitrary",)),
        name="ffn",
    )(mix, w_out_bf16, x, gain, w_gate_up_bf16, w_down_bf16)


def _group_rms(y, gmat, gain):
    d = y.shape[1]
    sq = (y * y).astype(BF16)
    ms = jnp.concatenate(
        [_dot(sq[:, s * LANE:(s + 1) * LANE], gmat) for s in range(d // LANE)], axis=1)
    return y * lax.rsqrt(ms + NORM_EPS) * gain


V_ROWS = LANE + 16


def _diff_in_kernel(x_ref, g_ref, w_ref, gm_ref, qn_ref, kn_ref, q_ref, k_ref, vt_ref):
    d = x_ref.shape[1]
    xn = _rms_rows(x_ref[...], g_ref[...]).astype(BF16)

    def proj(j):
        return _dot(xn, w_ref[:, j * d:(j + 1) * d])

    q_ref[...] = (_group_rms(proj(0), gm_ref[...], qn_ref[...]) * (DIFF_HEAD_DIM ** -0.5 * LOG2E)).astype(BF16)
    k_ref[...] = _group_rms(proj(1), gm_ref[...], kn_ref[...]).astype(BF16)
    v = proj(2)
    for h in range(d // LANE):
        vt_ref[h, :LANE, :] = v[:, h * LANE:(h + 1) * LANE].T.astype(BF16)
        vt_ref[h, LANE:, :] = jnp.ones((V_ROWS - LANE, v.shape[0]), BF16)


def _diff_in(x, gain, w_bf16, q_norm_row, k_norm_row, batch, seq):
    n, d = x.shape
    heads = d // LANE
    tm = min(TILES["tm_in"], seq)
    nt = seq // tm
    lane_group = jnp.arange(LANE) // DIFF_HEAD_DIM
    gmat = ((lane_group[:, None] == lane_group[None, :]).astype(F32) / DIFF_HEAD_DIM).astype(BF16)
    row = pl.BlockSpec((tm, d), lambda i: (i, 0))
    vec = pl.BlockSpec((1, d), lambda i: (0, 0))
    return pl.pallas_call(
        _diff_in_kernel,
        grid=(n // tm,),
        in_specs=[row, vec, pl.BlockSpec((d, 3 * d), lambda i: (0, 0)),
                  pl.BlockSpec((LANE, LANE), lambda i: (0, 0)), vec, vec],
        out_specs=[row, row, pl.BlockSpec((None, heads, V_ROWS, tm), lambda i: (i // nt, 0, 0, i % nt))],
        out_shape=[jax.ShapeDtypeStruct((n, d), BF16), jax.ShapeDtypeStruct((n, d), BF16),
                   jax.ShapeDtypeStruct((batch, heads, V_ROWS, seq), BF16)],
        compiler_params=_cparams(("arbitrary",)),
        name="diff_in",
    )(x, gain, w_bf16, gmat, q_norm_row, k_norm_row)


ATTN_UNROLL = 4
FULL, DIAG_A, DIAG_B = range(3)


def _attn_kernel(slope_ref, q_ref, k_ref, vt_ref, lam_ref, sn_ref, o_ref, m_ref, acc_ref, w_ref,
                 *, tq, lam_init):
    h = pl.program_id(1)
    qb = pl.program_id(2)
    slope = slope_ref[h] * LOG2E

    q = q_ref[...]
    lane = lax.broadcasted_iota(I32, q.shape, 1)
    zero = jnp.zeros_like(q)
    qs = jnp.concatenate([jnp.where(lane < DIFF_HEAD_DIM, q, zero),
                          jnp.where(lane >= DIFF_HEAD_DIM, q, zero)], axis=0)

    m_ref[...] = jnp.full_like(m_ref, -jnp.inf)
    acc_ref[...] = jnp.zeros_like(acc_ref)

    lane_k = lax.broadcasted_iota(I32, (tq, LANE), 1)
    bias = slope * lax.broadcasted_iota(I32, (tq, LANE), 0).astype(F32)
    bias_hi = bias.astype(BF16).astype(F32)
    bias_cols = jnp.where(lane_k == 0, bias_hi, jnp.where(lane_k == 1, bias - bias_hi, 0.0)).astype(BF16)
    lane_q = lax.broadcasted_iota(I32, (2 * tq, LANE), 1)
    qsx = jnp.concatenate([qs, jnp.where(lane_q < 2, 1.0, 0.0).astype(BF16)], axis=1)

    half = tq // 2
    qsx_b = jnp.concatenate([qsx[half:tq], qsx[tq + half:]], axis=0)

    def put(buf, kind, kb):
        k0 = pl.multiple_of(kb * tq, tq)
        if kind == FULL:
            w_ref[buf] = _dot_nt(jnp.concatenate([k_ref[pl.ds(k0, tq), :], bias_cols], axis=1), qsx)
        elif kind == DIAG_A:
            kx = jnp.concatenate([k_ref[pl.ds(k0, half), :], bias_cols[:half]], axis=1)
            w_ref[buf, :half, :] = _dot_nt(kx, qsx)
        else:
            kx = jnp.concatenate([k_ref[pl.ds(k0 + half, half), :], bias_cols[half:]], axis=1)
            w_ref[buf, :half, :tq] = _dot_nt(kx, qsx_b)

    def consume(buf, kind, kb):
        k0 = pl.multiple_of(kb * tq, tq)
        if kind == FULL:
            w = w_ref[buf]
            keys = pl.ds(k0, tq)
            off = slope * ((kb - qb) * tq).astype(F32)
        else:
            period = tq if kind == DIAG_A else half
            w = w_ref[buf, :half, :] if kind == DIAG_A else w_ref[buf, :half, :tq]
            keys = pl.ds(k0, half) if kind == DIAG_A else pl.ds(k0 + half, half)
            r = lax.broadcasted_iota(I32, w.shape, 0)
            c = lax.broadcasted_iota(I32, w.shape, 1)
            w = jnp.where(r <= jnp.where(c >= period, c - period, c), w, -jnp.inf)
            off = 0.0
        if kind == DIAG_B:
            m_old = jnp.concatenate([m_ref[:, half:tq], m_ref[:, tq + half:]], axis=1)
        else:
            m_old = m_ref[...]
        m_new = jnp.maximum(m_old, jnp.max(w, axis=0, keepdims=True) + off)
        alpha = jnp.exp2(m_old - m_new)
        p = jnp.exp2(w - (m_new - off)).astype(BF16)
        pv = _dot(vt_ref[:, keys], p)
        if kind == DIAG_B:
            acc_ref[:, half:tq] = alpha[:, :half] * acc_ref[:, half:tq] + pv[:, :half]
            acc_ref[:, tq + half:] = alpha[:, half:] * acc_ref[:, tq + half:] + pv[:, half:]
            m_ref[:, half:tq] = m_new[:, :half]
            m_ref[:, tq + half:] = m_new[:, half:]
        else:
            acc_ref[...] = alpha * acc_ref[...] + pv
            m_ref[...] = m_new

    def run(items):
        for idx, (kind, kb) in enumerate(items):
            if idx + 1 < len(items):
                put((idx + 1) % 2, *items[idx + 1])
            consume(idx % 2, kind, kb)

    @pl.when(qb == 0)
    def _():
        put(0, DIAG_A, qb)

    @pl.when(qb > 0)
    def _():
        put(0, FULL, 0)

    def group(g, carry):
        first = ATTN_UNROLL * g
        for u in range(ATTN_UNROLL):
            put((u + 1) % 2, FULL, first + u + 1)
            consume(u % 2, FULL, first + u)
        return carry

    n_groups = jnp.maximum(qb - 1, 0) // ATTN_UNROLL
    lax.fori_loop(0, n_groups, group, 0)
    rest = qb - ATTN_UNROLL * n_groups
    for r in range(ATTN_UNROLL + 1):
        @pl.when(rest == r)
        def _(r=r):
            run([(FULL, qb - r + u) for u in range(r)] + [(DIAG_A, qb), (DIAG_B, qb)])

    lam_rows = lam_ref[...]
    lam = (jnp.exp(jnp.sum(lam_rows[0:1] * lam_rows[1:2], axis=1, keepdims=True))
           - jnp.exp(jnp.sum(lam_rows[2:3] * lam_rows[3:4], axis=1, keepdims=True)) + lam_init)
    acc = acc_ref[...]
    on = acc[:LANE] / acc[LANE:LANE + 1]
    o = (on[:, :tq] - lam * on[:, tq:]).T
    o_ref[...] = (_rms_rows(o, sn_ref[...]) * (1.0 - lam_init)).astype(BF16)


def _attention(q, k, vt, slopes, lam_rows, sub_norm_row, batch, seq, lam_init):
    n, d = q.shape
    heads = d // LANE
    tq = min(TILES["tq"], seq)
    nq = seq // tq
    grid_spec = pltpu.PrefetchScalarGridSpec(
        num_scalar_prefetch=1,
        grid=(batch, heads, nq),
        in_specs=[
            pl.BlockSpec((tq, LANE), lambda b, h, i, s: (b * nq + i, h)),
            pl.BlockSpec((seq, LANE), lambda b, h, i, s: (b, h)),
            pl.BlockSpec((None, None, V_ROWS, seq), lambda b, h, i, s: (b, h, 0, 0)),
            pl.BlockSpec((8, LANE), lambda b, h, i, s: (0, 0)),
            pl.BlockSpec((1, LANE), lambda b, h, i, s: (0, 0)),
        ],
        out_specs=pl.BlockSpec((tq, LANE), lambda b, h, i, s: (b * nq + i, h)),
        scratch_shapes=[pltpu.VMEM((1, 2 * tq), F32), pltpu.VMEM((V_ROWS, 2 * tq), F32),
                        pltpu.VMEM((2, tq, 2 * tq), F32)],
    )
    return pl.pallas_call(
        functools.partial(_attn_kernel, tq=tq, lam_init=lam_init),
        grid_spec=grid_spec,
        out_shape=jax.ShapeDtypeStruct((n, d), BF16),
        compiler_params=_cparams(("arbitrary", "arbitrary", "arbitrary")),
        name="diff_attn",
    )(slopes, q, k, vt, lam_rows, sub_norm_row)


ROUTE_ROWS = 16
ROW_ALIGN = 8


def _proj_router_kernel(a_ref, w_ref, r_ref, g_ref, rw_ref, triu_ref, h_ref, hn_ref, meta_ref, cnt_ref,
                        *, n_experts):
    h = r_ref[...] + _dot(a_ref[...], w_ref[...])
    h_ref[...] = h
    hn = _rms_rows(h, g_ref[...])
    hn_hi = hn.astype(BF16)
    hn_ref[...] = hn_hi
    hn_lo = (hn - hn_hi.astype(F32)).astype(BF16)
    both = _dot(hn_hi, rw_ref[...])
    logits = both[:, :LANE] + both[:, LANE:] + _dot(hn_lo, rw_ref[:, :LANE])
    lt = logits.T[:ROUTE_ROWS]
    sub = lax.broadcasted_iota(I32, lt.shape, 0)
    subf = sub.astype(F32)
    neg = jnp.float32(-jnp.inf)
    lt = jnp.where(sub < n_experts, lt, neg)
    v1 = jnp.max(lt, axis=0, keepdims=True)
    i1 = jnp.min(jnp.where(lt == v1, subf, float(ROUTE_ROWS)), axis=0, keepdims=True)
    oh1 = subf == i1
    rest = jnp.where(oh1, neg, lt)
    v2 = jnp.max(rest, axis=0, keepdims=True)
    i2 = jnp.min(jnp.where(rest == v2, subf, float(ROUTE_ROWS)), axis=0, keepdims=True)
    oh2 = subf == i2
    e = jnp.exp(v2 - v1)
    w1 = 1.0 / (1.0 + e)
    w2 = e * w1
    cnt = jnp.where(oh1, 1.0, 0.0) + jnp.where(oh2, 1.0, 0.0)
    before = _dot(cnt.astype(BF16), triu_ref[...])
    counts = jnp.sum(cnt, axis=1, keepdims=True) + jnp.zeros((ROUTE_ROWS, LANE), F32)
    padded = jnp.floor((counts + (ROW_ALIGN - 1)) * (1.0 / ROW_ALIGN)) * ROW_ALIGN
    er = lax.broadcasted_iota(I32, (ROUTE_ROWS, ROUTE_ROWS), 0)
    ec = lax.broadcasted_iota(I32, (ROUTE_ROWS, ROUTE_ROWS), 1)
    seg = _dot(jnp.where(ec < er, 1.0, 0.0).astype(BF16), padded.astype(BF16))[:, 0:1]
    s1 = jnp.sum(jnp.where(oh1, before + seg, 0.0), axis=0, keepdims=True)
    s2 = jnp.sum(jnp.where(oh2, before + seg, 0.0), axis=0, keepdims=True)
    meta = jnp.where(sub == 0, i1, jnp.where(sub == 1, i2, jnp.where(sub == 2, w1, jnp.where(
        sub == 3, w2, jnp.where(sub == 4, s1, jnp.where(sub == 5, s2, 0.0))))))
    meta_ref[...] = meta[:8]
    cnt_ref[...] = counts


def _proj_router(a, w_bf16, res, gain, router):
    n, d = res.shape
    n_experts = router.shape[1]
    assert n_experts <= ROUTE_ROWS
    tm = min(TILES["tm_router"], n)
    rw = jnp.zeros((d, LANE), F32).at[:, :n_experts].set(router.astype(F32))
    rw_hi = rw.astype(BF16)
    rw_cat = jnp.concatenate([rw_hi, (rw - rw_hi.astype(F32)).astype(BF16)], axis=1)
    triu = jnp.triu(jnp.ones((tm, tm), F32), 1).astype(BF16)
    row = pl.BlockSpec((tm, d), lambda i: (i, 0))
    return pl.pallas_call(
        functools.partial(_proj_router_kernel, n_experts=n_experts),
        grid=(n // tm,),
        in_specs=[row, pl.BlockSpec((d, d), lambda i: (0, 0)), row,
                  pl.BlockSpec((1, d), lambda i: (0, 0)),
                  pl.BlockSpec((d, 2 * LANE), lambda i: (0, 0)),
                  pl.BlockSpec((tm, tm), lambda i: (0, 0))],
        out_specs=[row, row, pl.BlockSpec((8, tm), lambda i: (0, i)),
                   pl.BlockSpec((ROUTE_ROWS, LANE), lambda i: (i, 0))],
        out_shape=[jax.ShapeDtypeStruct((n, d), F32), jax.ShapeDtypeStruct((n, d), BF16),
                   jax.ShapeDtypeStruct((8, n), F32),
                   jax.ShapeDtypeStruct((n // tm * ROUTE_ROWS, LANE), F32)],
        compiler_params=_cparams(("arbitrary",)),
        name="proj_router",
    )(a, w_bf16, res, gain, rw_cat, triu)


def _for_each_piece(n_rows, max_rows, fn):
    size = ROW_ALIGN
    while size * 2 <= max_rows:
        size *= 2
    done = 0
    while size >= ROW_ALIGN:
        take = (n_rows & size) != 0

        @pl.when(take)
        def _(done=done, size=size):
            fn(done, size)

        done = done + jnp.where(take, size, 0)
        size //= 2


def _segment_copies(pc_ref, seg_ref, dst_ref, tile, n_experts, max_rows, make_copy, start):
    for e in range(n_experts):
        idx = tile * n_experts + e
        local0 = seg_ref[idx]
        sorted0 = dst_ref[idx]

        def piece(off, size, local0=local0, sorted0=sorted0):
            cp = make_copy(pl.multiple_of(local0 + off, ROW_ALIGN), pl.multiple_of(sorted0 + off, ROW_ALIGN), size)
            if start:
                cp.start()
            else:
                cp.wait()

        _for_each_piece(pc_ref[idx], max_rows, piece)


def _slot_one_hot(meta_ref, n_slots, gated):
    tokens = meta_ref.shape[1]
    sub = lax.broadcasted_iota(I32, (n_slots, tokens), 0)
    out = None
    for kk in range(TOP_K):
        slot = meta_ref[4 + kk:5 + kk, :].astype(I32)
        val = meta_ref[2 + kk:3 + kk, :] if gated else 1.0
        term = jnp.where(sub == slot, val, 0.0)
        out = term if out is None else out + term
    return out.astype(BF16)


ZERO_ROWS = 512


def _scatter_kernel(pc_ref, seg_ref, dst_ref, gap_ref, meta_ref, hn_ref, xs_ref, sorted_ref, zeros_ref, sem, zsem,
                    *, n_experts, n_slots, tmx, max_empty_tiles):
    tile = pl.program_id(0)
    last = pl.num_programs(0) - 1
    max_rows = hn_ref.shape[0]

    def copies(t, b, start):
        def make_copy(local, glob, size):
            return pltpu.make_async_copy(sorted_ref.at[b, pl.ds(local, size)], xs_ref.at[pl.ds(glob, size)],
                                         sem.at[b])
        _segment_copies(pc_ref, seg_ref, dst_ref, t, n_experts, max_rows, make_copy, start)

    def zero_fill(start):
        def go(row0, size):
            cp = pltpu.make_async_copy(zeros_ref.at[pl.ds(0, size)],
                                       xs_ref.at[pl.ds(pl.multiple_of(row0, ROW_ALIGN), size)], zsem)
            if start:
                cp.start()
            else:
                cp.wait()

        zrows = zeros_ref.shape[0]
        for e in range(n_experts):
            row0 = gap_ref[2 * e]
            _for_each_piece(gap_ref[2 * e + 1], zrows, lambda off, size, row0=row0: go(row0 + off, size))
        for t in range(max_empty_tiles):
            @pl.when(t < gap_ref[2 * n_experts + 1])
            def _(t=t):
                for r0 in range(0, tmx, zrows):
                    go(gap_ref[2 * n_experts] + t * tmx + r0, zrows)

    @pl.when(tile == 0)
    def _():
        zeros_ref[...] = jnp.zeros_like(zeros_ref)
        zero_fill(True)
        zero_fill(False)

    one_hot = _slot_one_hot(meta_ref, n_slots, False)
    for b in range(2):
        @pl.when(tile % 2 == b)
        def _(b=b):
            sorted_ref[b] = _dot(one_hot, hn_ref[...])
            copies(tile, b, True)

            @pl.when(tile > 0)
            def _():
                copies(tile - 1, 1 - b, False)

            @pl.when(tile == last)
            def _():
                copies(tile, b, False)


def _scatter_rows(hn, meta, pc, seg, dst, gaps, n_sorted, tm, tmx, n_experts):
    n, d = hn.shape
    n_slots = TOP_K * tm + ROW_ALIGN * n_experts
    assert tmx % ZERO_ROWS == 0 or tmx < ZERO_ROWS
    grid_spec = pltpu.PrefetchScalarGridSpec(
        num_scalar_prefetch=4,
        grid=(n // tm,),
        in_specs=[pl.BlockSpec((8, tm), lambda i, *_: (0, i)),
                  pl.BlockSpec((tm, d), lambda i, *_: (i, 0))],
        out_specs=pl.BlockSpec(memory_space=pl.ANY),
        scratch_shapes=[pltpu.VMEM((2, n_slots, d), F32), pltpu.VMEM((min(ZERO_ROWS, tmx), d), F32),
                        pltpu.SemaphoreType.DMA((2,)), pltpu.SemaphoreType.DMA(())],
    )
    return pl.pallas_call(
        functools.partial(_scatter_kernel, n_experts=n_experts, n_slots=n_slots, tmx=tmx,
                          max_empty_tiles=n_sorted // tmx - (TOP_K * n) // tmx),
        grid_spec=grid_spec,
        out_shape=jax.ShapeDtypeStruct((n_sorted, d), F32),
        compiler_params=pltpu.CompilerParams(dimension_semantics=("arbitrary",),
                                             vmem_limit_bytes=VMEM_LIMIT, has_side_effects=True),
        name="moe_scatter",
    )(pc, seg, dst, gaps, meta, hn)


EXPERT_SUB_ROWS = 512


def _expert_ffn_kernel(te_ref, na_ref, x_ref, wg_ref, wu_ref, wd_ref, y_ref, xb_ref, acc_ref):
    del te_ref
    i = pl.program_id(0)
    j = pl.program_id(1)

    @pl.when(i < na_ref[0])
    def _():
        @pl.when(j == 0)
        def _():
            xb_ref[...] = x_ref[...].astype(BF16)
            acc_ref[...] = jnp.zeros_like(acc_ref)

        sub_rows = min(EXPERT_SUB_ROWS, xb_ref.shape[0])
        for r0 in range(0, xb_ref.shape[0], sub_rows):
            rows = slice(r0, r0 + sub_rows)
            xb = xb_ref[rows, :]
            act = (_silu(_dot(xb, wg_ref[...])) * _dot(xb, wu_ref[...])).astype(BF16)
            acc_ref[rows, :] += _dot(act, wd_ref[...])

        @pl.when(j == pl.num_programs(1) - 1)
        def _():
            y_ref[...] = acc_ref[...]

    @pl.when((i >= na_ref[0]) & (j == pl.num_programs(1) - 1))
    def _():
        y_ref[...] = jnp.zeros_like(y_ref)


def _expert_ffn(xs, tile_expert, n_active, w_gate_up_bf16, w_down_bf16, tmx):
    n_sorted, d = xs.shape
    f = w_down_bf16.shape[1]
    fb = _hidden_block(f, TILES["fbx"])
    nf = f // fb
    n_tiles = n_sorted // tmx

    def row_map(i, j, te, na):
        return (jnp.minimum(i, na[0] - 1), 0)

    def col(i, j, na):
        return jnp.where(i < na[0], j, nf - 1)

    grid_spec = pltpu.PrefetchScalarGridSpec(
        num_scalar_prefetch=2,
        grid=(n_tiles, nf),
        in_specs=[
            pl.BlockSpec((tmx, d), row_map),
            pl.BlockSpec((None, d, fb), lambda i, j, te, na: (te[i], 0, col(i, j, na))),
            pl.BlockSpec((None, d, fb), lambda i, j, te, na: (te[i], 0, col(i, j, na) + nf)),
            pl.BlockSpec((None, fb, d), lambda i, j, te, na: (te[i], col(i, j, na), 0)),
        ],
        out_specs=pl.BlockSpec((tmx, d), lambda i, j, te, na: (i, 0)),
        scratch_shapes=[pltpu.VMEM((tmx, d), BF16), pltpu.VMEM((tmx, d), F32)],
    )
    return pl.pallas_call(
        _expert_ffn_kernel,
        grid_spec=grid_spec,
        out_shape=jax.ShapeDtypeStruct((n_sorted, d), F32),
        compiler_params=_cparams(("arbitrary", "arbitrary")),
        name="expert_ffn",
    )(tile_expert, n_active, xs, w_gate_up_bf16, w_gate_up_bf16, w_down_bf16)


def _combine_kernel(pc_ref, seg_ref, dst_ref, meta_ref, h_ref, ys_ref, o_ref, buf_ref, sem,
                    *, n_experts, n_slots):
    tile = pl.program_id(0)
    last = pl.num_programs(0) - 1
    max_rows = h_ref.shape[0]

    def copies(t, b, start):
        def make_copy(local, glob, size):
            return pltpu.make_async_copy(ys_ref.at[pl.ds(glob, size)], buf_ref.at[b, pl.ds(local, size)],
                                         sem.at[b])
        _segment_copies(pc_ref, seg_ref, dst_ref, t, n_experts, max_rows, make_copy, start)

    @pl.when(tile == 0)
    def _():
        buf_ref[...] = jnp.zeros_like(buf_ref)
        copies(0, 0, True)

    gates = _slot_one_hot(meta_ref, n_slots, True)
    for b in range(2):
        @pl.when(tile % 2 == b)
        def _(b=b):
            @pl.when(tile < last)
            def _():
                copies(tile + 1, 1 - b, True)

            copies(tile, b, False)
            o_ref[...] = h_ref[...] + _dot_tn(gates, buf_ref[b].astype(BF16))


def _combine(h, meta, pc, seg, dst, ys, tm, n_experts):
    n, d = h.shape
    n_slots = TOP_K * tm + ROW_ALIGN * n_experts
    grid_spec = pltpu.PrefetchScalarGridSpec(
        num_scalar_prefetch=3,
        grid=(n // tm,),
        in_specs=[pl.BlockSpec((8, tm), lambda i, *_: (0, i)),
                  pl.BlockSpec((tm, d), lambda i, *_: (i, 0)),
                  pl.BlockSpec(memory_space=pl.ANY)],
        out_specs=pl.BlockSpec((tm, d), lambda i, *_: (i, 0)),
        scratch_shapes=[pltpu.VMEM((2, n_slots, d), F32), pltpu.SemaphoreType.DMA((2,))],
    )
    return pl.pallas_call(
        functools.partial(_combine_kernel, n_experts=n_experts, n_slots=n_slots),
        grid_spec=grid_spec,
        out_shape=jax.ShapeDtypeStruct((n, d), F32),
        compiler_params=_cparams(("arbitrary",)),
        name="moe_combine",
    )(pc, seg, dst, meta, h, ys)


def _row(vec):
    return vec.astype(F32).reshape(1, -1)


def _round_up(x, m):
    return (x + m - 1) // m * m


def kernel(x, lower_bounds, l0_mix_norm, l0_hgrn_w_in, l0_hgrn_out_norm, l0_hgrn_w_out, l0_ffn_norm, l0_ffn_w_gate_up, l0_ffn_w_down, l1_mix_norm, l1_diff_w_in, l1_q_norm, l1_k_norm, l1_lambda_q1, l1_lambda_k1, l1_lambda_q2, l1_lambda_k2, l1_diff_sub_norm, l1_diff_w_out, l1_ffn_norm, l1_router, l1_moe_w_gate_up, l1_moe_w_down):
    batch, seq, d = x.shape
    n = batch * seq
    heads = d // LANE
    n_experts = l1_router.shape[1]
    x2 = x.reshape(n, d).astype(F32)

    lb0 = jnp.cumsum(jax.nn.softmax(lower_bounds.astype(F32), axis=0), axis=0)[0]
    q, k, lf, v, gs = _hgrn_in(x2, _row(l0_mix_norm), _row(lb0), l0_hgrn_w_in.astype(BF16))
    og = _hgrn_rec(q, k, lf, v, gs, _row(l0_hgrn_out_norm), batch, seq)
    h = _ffn(og, l0_hgrn_w_out.astype(BF16), x2, _row(l0_ffn_norm),
             l0_ffn_w_gate_up.astype(BF16), l0_ffn_w_down.astype(BF16))

    lam_init = 0.8 - 0.6 * math.exp(-0.3 * 1)
    q, k, vt = _diff_in(h, _row(l1_mix_norm), l1_diff_w_in.astype(BF16),
                        _row(jnp.tile(l1_q_norm, 2 * heads)), _row(jnp.tile(l1_k_norm, 2 * heads)), batch, seq)
    slopes = 2.0 ** (-8.0 * jnp.arange(1, heads + 1, dtype=F32) / heads)
    lam_rows = jnp.zeros((8, LANE), F32).at[:4, :DIFF_HEAD_DIM].set(
        jnp.stack([l1_lambda_q1, l1_lambda_k1, l1_lambda_q2, l1_lambda_k2]).astype(F32))
    oa = _attention(q, k, vt, slopes, lam_rows, _row(l1_diff_sub_norm), batch, seq, lam_init)

    h, hn, meta, tile_cnt = _proj_router(oa, l1_diff_w_out.astype(BF16), h, _row(l1_ffn_norm), l1_router)
    tm = min(TILES["tm_router"], n)
    n_tok_tiles = n // tm
    tmx = min(TILES["tmx"], n)
    cnt = tile_cnt.reshape(n_tok_tiles, ROUTE_ROWS, LANE)[:, :n_experts, 0].astype(I32)
    pc = _round_up(cnt, ROW_ALIGN)
    seg = jnp.cumsum(pc, axis=1) - pc
    region = _round_up(jnp.sum(pc, axis=0), tmx)
    region_end = jnp.cumsum(region)
    dst = (region_end - region)[None, :] + jnp.cumsum(pc, axis=0) - pc
    n_x_tiles = (TOP_K * n + n_tok_tiles * n_experts * (ROW_ALIGN - 1)) // tmx + n_experts
    n_active = (region_end[-1:] // tmx).astype(I32)
    x_tile_start = jnp.arange(n_x_tiles, dtype=I32) * tmx
    tile_expert = jnp.sum((x_tile_start[:, None] >= region_end[None, :]).astype(I32), axis=1)
    tile_expert = jnp.minimum(tile_expert, tile_expert[n_active[0] - 1]).astype(I32)
    pc_f, seg_f, dst_f = (t.reshape(-1).astype(I32) for t in (pc, seg, dst))
    used = jnp.sum(pc, axis=0)
    gaps = jnp.concatenate([jnp.stack([region_end - region + used, region - used], axis=1).reshape(-1),
                            region_end[-1:], n_x_tiles - n_active]).astype(I32)

    xs = _scatter_rows(hn, meta, pc_f, seg_f, dst_f, gaps, n_x_tiles * tmx, tm, tmx, n_experts)
    ys = _expert_ffn(xs, tile_expert, n_active, l1_moe_w_gate_up.astype(BF16),
                     l1_moe_w_down.astype(BF16), tmx)
    out = _combine(h, meta, pc_f, seg_f, dst_f, ys, tm, n_experts)
    return out.reshape(batch, seq, d).astype(x.dtype)
```

```python
import functools
import math

import jax
import jax.numpy as jnp
from jax import lax
from jax.experimental import pallas as pl
from jax.experimental.pallas import tpu as pltpu

F32 = jnp.float32
BF16 = jnp.bfloat16
I32 = jnp.int32

NORM_EPS = 1e-6
LANE = 128
HGRN_CHUNK = 64
DIFF_HEAD_DIM = 64
TOP_K = 2
LOG2E = math.log2(math.e)
EXP_CLAMP = 60.0
VMEM_LIMIT = 56 * 1024 * 1024

TILES = dict(
    tm_in=512,
    tb_rec=512,
    tm_ffn=512,
    tq=1024,
    tm_router=512,
    tmx=1024,
    fbx=512,
)


def _cparams(sem):
    return pltpu.CompilerParams(dimension_semantics=sem, vmem_limit_bytes=VMEM_LIMIT)


def _sigmoid(x):
    return 1.0 / (1.0 + jnp.exp(-x))


def _silu(x):
    return x * _sigmoid(x)


def _rms_rows(x, gain):
    ms = jnp.mean(x * x, axis=-1, keepdims=True)
    return x * lax.rsqrt(ms + NORM_EPS) * gain


def _dot(a, b):
    return jnp.dot(a, b, preferred_element_type=F32)


def _dot_nt(a, b):
    return lax.dot_general(a, b, (((1,), (1,)), ((), ())), preferred_element_type=F32)


def _dot_tn(a, b):
    return lax.dot_general(a, b, (((0,), (0,)), ((), ())), preferred_element_type=F32)


def _hgrn_in_kernel(x_ref, g_ref, lb_ref, w_ref, q_ref, k_ref, lf_ref, v_ref, gs_ref):
    d = x_ref.shape[1]
    xn = _rms_rows(x_ref[...], g_ref[...]).astype(BF16)

    def proj(j):
        return _dot(xn, w_ref[:, j * d:(j + 1) * d])

    q_ref[...] = _silu(proj(0)).astype(BF16)
    lb = lb_ref[...]
    forget = lb + (1.0 - lb) * _sigmoid(proj(1))
    k_ref[...] = (1.0 - forget).astype(BF16)
    lf_ref[...] = jnp.log(forget)
    v_ref[...] = proj(2).astype(BF16)
    gs_ref[...] = _silu(proj(3)).astype(BF16)


def _hgrn_in(x, gain, lb, w_bf16):
    n, d = x.shape
    tm = min(TILES["tm_in"], n)
    row = pl.BlockSpec((tm, d), lambda i: (i, 0))
    vec = pl.BlockSpec((1, d), lambda i: (0, 0))
    return pl.pallas_call(
        _hgrn_in_kernel,
        grid=(n // tm,),
        in_specs=[row, vec, vec, pl.BlockSpec((d, 4 * d), lambda i: (0, 0))],
        out_specs=[row, row, row, row, row],
        out_shape=[
            jax.ShapeDtypeStruct((n, d), BF16),
            jax.ShapeDtypeStruct((n, d), BF16),
            jax.ShapeDtypeStruct((n, d), F32),
            jax.ShapeDtypeStruct((n, d), BF16),
            jax.ShapeDtypeStruct((n, d), BF16),
        ],
        compiler_params=_cparams(("arbitrary",)),
        name="hgrn_in",
    )(x, gain, lb, w_bf16)


def _hgrn_rec_kernel(q_ref, k_ref, lf_ref, v_ref, gs_ref, on_ref, o_ref, st_ref, *, chunk, heads):
    @pl.when(pl.program_id(1) == 0)
    def _():
        st_ref[...] = jnp.zeros_like(st_ref)

    c = chunk
    n_chunks = q_ref.shape[0] // c
    row = lax.broadcasted_iota(I32, (c, c), 0)
    col = lax.broadcasted_iota(I32, (c, c), 1)
    causal = col <= row
    tri = causal.astype(BF16)
    mid = c // 2 - 1

    def body(ci, carry):
        r0 = pl.multiple_of(ci * c, c)
        rows = pl.ds(r0, c)
        lf = lf_ref[rows, :]
        hi = lf.astype(BF16)
        lo = (lf - hi.astype(F32)).astype(BF16)
        b = _dot(tri, hi) + _dot(tri, lo)
        bmid = b[mid:mid + 1, :]
        bend = b[c - 1:c, :]
        e_q = jnp.exp(jnp.minimum(b - bmid, EXP_CLAMP))
        e_k = jnp.exp(jnp.minimum(bmid - b, EXP_CLAMP))
        qt = (q_ref[rows, :].astype(F32) * e_q).astype(BF16)
        kt = (k_ref[rows, :].astype(F32) * e_k).astype(BF16)
        v = v_ref[rows, :]
        gs = gs_ref[rows, :].astype(F32)
        e_mid = jnp.exp(bmid)
        e_end = jnp.exp(bend)
        e_end_mid = jnp.exp(bend - bmid)
        for h in range(heads):
            sl = slice(h * LANE, (h + 1) * LANE)
            qh, kh, vh = qt[:, sl], kt[:, sl], v[:, sl]
            a = jnp.where(causal, _dot_nt(qh, kh), 0.0).astype(BF16)
            st = st_ref[h]
            o = _dot(a, vh) + _dot_nt(qh, (st * e_mid[:, sl]).astype(BF16))
            st_ref[h] = st * e_end[:, sl] + _dot_tn(vh, kh) * e_end_mid[:, sl]
            og = _rms_rows(o, on_ref[:, sl]) * gs[:, sl]
            o_ref[rows, sl] = og.astype(BF16)
        return carry

    lax.fori_loop(0, n_chunks, body, 0, unroll=True)


def _hgrn_rec(q, k, lf, v, gs, out_norm, batch, seq):
    n, d = q.shape
    heads = d // LANE
    tb = min(TILES["tb_rec"], seq)
    nt = seq // tb
    blk = pl.BlockSpec((tb, d), lambda b, t: (b * nt + t, 0))
    return pl.pallas_call(
        functools.partial(_hgrn_rec_kernel, chunk=HGRN_CHUNK, heads=heads),
        grid=(batch, nt),
        in_specs=[blk, blk, blk, blk, blk, pl.BlockSpec((1, d), lambda b, t: (0, 0))],
        out_specs=blk,
        out_shape=jax.ShapeDtypeStruct((n, d), BF16),
        scratch_shapes=[pltpu.VMEM((heads, LANE, LANE), F32)],
        compiler_params=_cparams(("arbitrary", "arbitrary")),
        name="hgrn_rec",
    )(q, k, lf, v, gs, out_norm)


def _ffn_kernel(a_ref, wo_ref, x_ref, g_ref, wgu_ref, wd_ref, o_ref):
    f = wd_ref.shape[0]
    h = x_ref[...] + _dot(a_ref[...], wo_ref[...])
    xn = _rms_rows(h, g_ref[...]).astype(BF16)
    gu = _dot(xn, wgu_ref[...])
    act = (_silu(gu[:, :f]) * gu[:, f:]).astype(BF16)
    o_ref[...] = h + _dot(act, wd_ref[...])


def _hidden_block(f, want):
    best = None
    for fb in range(LANE, f + 1, LANE):
        if f % fb == 0 and fb <= want:
            best = fb
    assert best is not None, f
    return best


def _ffn(mix, w_out_bf16, x, gain, w_gate_up_bf16, w_down_bf16):
    n, d = x.shape
    f = w_down_bf16.shape[0]
    tm = min(TILES["tm_ffn"], n)
    row = pl.BlockSpec((tm, d), lambda i: (i, 0))
    once = pl.Buffered(1)
    return pl.pallas_call(
        _ffn_kernel,
        grid=(n // tm,),
        in_specs=[
            row,
            pl.BlockSpec((d, d), lambda i: (0, 0), pipeline_mode=once),
            row,
            pl.BlockSpec((1, d), lambda i: (0, 0)),
            pl.BlockSpec((d, 2 * f), lambda i: (0, 0), pipeline_mode=once),
            pl.BlockSpec((f, d), lambda i: (0, 0), pipeline_mode=once),
        ],
        out_specs=row,
        out_shape=jax.ShapeDtypeStruct((n, d), F32),
        compiler_params=_cparams(("arbitrary",)),
        name="ffn",
    )(mix, w_out_bf16, x, gain, w_gate_up_bf16, w_down_bf16)


def _group_rms(y, gmat, gain):
    d = y.shape[1]
    sq = (y * y).astype(BF16)
    ms = jnp.concatenate(
        [_dot(sq[:, s * LANE:(s + 1) * LANE], gmat) for s in range(d // LANE)], axis=1)
    return y * lax.rsqrt(ms + NORM_EPS) * gain


V_ROWS = LANE + 16


def _diff_in_kernel(x_ref, g_ref, w_ref, gm_ref, qn_ref, kn_ref, q_ref, k_ref, vt_ref):
    d = x_ref.shape[1]
    xn = _rms_rows(x_ref[...], g_ref[...]).astype(BF16)

    def proj(j):
        return _dot(xn, w_ref[:, j * d:(j + 1) * d])

    q_ref[...] = (_group_rms(proj(0), gm_ref[...], qn_ref[...]) * (DIFF_HEAD_DIM ** -0.5 * LOG2E)).astype(BF16)
    k_ref[...] = _group_rms(proj(1), gm_ref[...], kn_ref[...]).astype(BF16)
    v = proj(2)
    for h in range(d // LANE):
        vt_ref[h, :LANE, :] = v[:, h * LANE:(h + 1) * LANE].T.astype(BF16)
        vt_ref[h, LANE:, :] = jnp.ones((V_ROWS - LANE, v.shape[0]), BF16)


def _diff_in(x, gain, w_bf16, q_norm_row, k_norm_row, batch, seq):
    n, d = x.shape
    heads = d // LANE
    tm = min(TILES["tm_in"], seq)
    nt = seq // tm
    lane_group = jnp.arange(LANE) // DIFF_HEAD_DIM
    gmat = ((lane_group[:, None] == lane_group[None, :]).astype(F32) / DIFF_HEAD_DIM).astype(BF16)
    row = pl.BlockSpec((tm, d), lambda i: (i, 0))
    vec = pl.BlockSpec((1, d), lambda i: (0, 0))
    return pl.pallas_call(
        _diff_in_kernel,
        grid=(n // tm,),
        in_specs=[row, vec, pl.BlockSpec((d, 3 * d), lambda i: (0, 0)),
                  pl.BlockSpec((LANE, LANE), lambda i: (0, 0)), vec, vec],
        out_specs=[row, row, pl.BlockSpec((None, heads, V_ROWS, tm), lambda i: (i // nt, 0, 0, i % nt))],
        out_shape=[jax.ShapeDtypeStruct((n, d), BF16), jax.ShapeDtypeStruct((n, d), BF16),
                   jax.ShapeDtypeStruct((batch, heads, V_ROWS, seq), BF16)],
        compiler_params=_cparams(("arbitrary",)),
        name="diff_in",
    )(x, gain, w_bf16, gmat, q_norm_row, k_norm_row)


ATTN_UNROLL = 4
FULL, DIAG_A, DIAG_B = range(3)


def _attn_kernel(slope_ref, q_ref, k_ref, vt_ref, lam_ref, sn_ref, o_ref, m_ref, acc_ref, w_ref,
                 *, tq, lam_init):
    h = pl.program_id(1)
    qb = pl.program_id(2)
    slope = slope_ref[h] * LOG2E

    q = q_ref[...]
    lane = lax.broadcasted_iota(I32, q.shape, 1)
    zero = jnp.zeros_like(q)
    qs = jnp.concatenate([jnp.where(lane < DIFF_HEAD_DIM, q, zero),
                          jnp.where(lane >= DIFF_HEAD_DIM, q, zero)], axis=0)

    nq2 = 2 * tq
    m_ref[...] = jnp.full_like(m_ref, -jnp.inf)
    acc_ref[...] = jnp.zeros_like(acc_ref)

    lane_k = lax.broadcasted_iota(I32, (tq, LANE), 1)
    bias = slope * lax.broadcasted_iota(I32, (tq, LANE), 0).astype(F32)
    bias_hi = bias.astype(BF16).astype(F32)
    bias_cols = jnp.where(lane_k == 0, bias_hi, jnp.where(lane_k == 1, bias - bias_hi, 0.0)).astype(BF16)
    lane_q = lax.broadcasted_iota(I32, (2 * tq, LANE), 1)
    qsx = jnp.concatenate([qs, jnp.where(lane_q < 2, 1.0, 0.0).astype(BF16)], axis=1)

    half = tq // 2
    qsx_b = jnp.concatenate([qsx[half:tq], qsx[tq + half:]], axis=0)

    def put(buf, kind, kb):
        k0 = pl.multiple_of(kb * tq, tq)
        if kind == FULL:
            w_ref[buf, :, :nq2] = _dot_nt(jnp.concatenate([k_ref[pl.ds(k0, tq), :], bias_cols], axis=1), qsx)
        elif kind == DIAG_A:
            kx = jnp.concatenate([k_ref[pl.ds(k0, half), :], bias_cols[:half]], axis=1)
            w_ref[buf, :half, :nq2] = _dot_nt(kx, qsx)
        else:
            kx = jnp.concatenate([k_ref[pl.ds(k0 + half, half), :], bias_cols[half:]], axis=1)
            w_ref[buf, :half, :tq] = _dot_nt(kx, qsx_b)

    def consume(buf, kind, kb):
        k0 = pl.multiple_of(kb * tq, tq)
        if kind == FULL:
            w = w_ref[buf, :, :nq2]
            keys = pl.ds(k0, tq)
            off = slope * ((kb - qb) * tq).astype(F32)
        else:
            period = tq if kind == DIAG_A else half
            w = w_ref[buf, :half, :nq2] if kind == DIAG_A else w_ref[buf, :half, :tq]
            keys = pl.ds(k0, half) if kind == DIAG_A else pl.ds(k0 + half, half)
            r = lax.broadcasted_iota(I32, w.shape, 0)
            c = lax.broadcasted_iota(I32, w.shape, 1)
            w = jnp.where(r <= jnp.where(c >= period, c - period, c), w, -jnp.inf)
            off = 0.0
        if kind == DIAG_B:
            m_old = jnp.concatenate([m_ref[:, half:tq], m_ref[:, tq + half:]], axis=1)
        else:
            m_old = m_ref[...]
        m_new = jnp.maximum(m_old, jnp.max(w, axis=0, keepdims=True) + off)
        alpha = jnp.exp2(m_old - m_new)
        p = jnp.exp2(w - (m_new - off)).astype(BF16)
        pv = _dot(vt_ref[:, keys], p)
        if kind == DIAG_B:
            acc_ref[:, half:tq] = alpha[:, :half] * acc_ref[:, half:tq] + pv[:, :half]
            acc_ref[:, tq + half:nq2] = alpha[:, half:] * acc_ref[:, tq + half:nq2] + pv[:, half:]
            m_ref[:, half:tq] = m_new[:, :half]
            m_ref[:, tq + half:] = m_new[:, half:]
        else:
            acc_ref[:, :nq2] = alpha * acc_ref[:, :nq2] + pv
            m_ref[...] = m_new

    put(0, DIAG_B, qb)

    @pl.when(qb == 0)
    def _():
        put(1, DIAG_A, qb)
        consume(0, DIAG_B, qb)
        consume(1, DIAG_A, qb)

    @pl.when(qb > 0)
    def _():
        put(1, FULL, 0)
        consume(0, DIAG_B, qb)

    def group(g, carry):
        first = ATTN_UNROLL * g
        for u in range(ATTN_UNROLL):
            put(u % 2, FULL, first + u + 1)
            consume((u + 1) % 2, FULL, first + u)
        return carry

    n_groups = jnp.maximum(qb - 1, 0) // ATTN_UNROLL
    lax.fori_loop(0, n_groups, group, 0)
    rest = qb - ATTN_UNROLL * n_groups
    for r in range(1, ATTN_UNROLL + 1):
        @pl.when(rest == r)
        def _(r=r):
            for u in range(r):
                if u + 1 < r:
                    put(u % 2, FULL, qb - r + u + 1)
                else:
                    put(u % 2, DIAG_A, qb)
                consume((u + 1) % 2, FULL, qb - r + u)
            consume((r + 1) % 2, DIAG_A, qb)

    lam_rows = lam_ref[...]
    lam = (jnp.exp(jnp.sum(lam_rows[0:1] * lam_rows[1:2], axis=1, keepdims=True))
           - jnp.exp(jnp.sum(lam_rows[2:3] * lam_rows[3:4], axis=1, keepdims=True)) + lam_init)
    acc = acc_ref[:, :nq2]
    on = acc[:LANE] / acc[LANE:LANE + 1]
    o = (on[:, :tq] - lam * on[:, tq:]).T
    o_ref[...] = (_rms_rows(o, sn_ref[...]) * (1.0 - lam_init)).astype(BF16)


def _attention(q, k, vt, slopes, lam_rows, sub_norm_row, batch, seq, lam_init):
    n, d = q.shape
    heads = d // LANE
    tq = min(TILES["tq"], seq)
    nq = seq // tq
    grid_spec = pltpu.PrefetchScalarGridSpec(
        num_scalar_prefetch=1,
        grid=(batch, heads, nq),
        in_specs=[
            pl.BlockSpec((tq, LANE), lambda b, h, i, s: (b * nq + i, h)),
            pl.BlockSpec((seq, LANE), lambda b, h, i, s: (b, h)),
            pl.BlockSpec((None, None, V_ROWS, seq), lambda b, h, i, s: (b, h, 0, 0)),
            pl.BlockSpec((8, LANE), lambda b, h, i, s: (0, 0)),
            pl.BlockSpec((1, LANE), lambda b, h, i, s: (0, 0)),
        ],
        out_specs=pl.BlockSpec((tq, LANE), lambda b, h, i, s: (b * nq + i, h)),
        scratch_shapes=[pltpu.VMEM((1, 2 * tq), F32), pltpu.VMEM((V_ROWS, 2 * tq + LANE), F32),
                        pltpu.VMEM((2, tq, 2 * tq + LANE), F32)],
    )
    return pl.pallas_call(
        functools.partial(_attn_kernel, tq=tq, lam_init=lam_init),
        grid_spec=grid_spec,
        out_shape=jax.ShapeDtypeStruct((n, d), BF16),
        compiler_params=_cparams(("arbitrary", "arbitrary", "arbitrary")),
        name="diff_attn",
    )(slopes, q, k, vt, lam_rows, sub_norm_row)


ROUTE_ROWS = 16
ROW_ALIGN = 8


def _proj_router_kernel(a_ref, w_ref, r_ref, g_ref, rw_ref, triu_ref, h_ref, hn_ref, meta_ref, cnt_ref,
                        *, n_experts):
    h = r_ref[...] + _dot(a_ref[...], w_ref[...])
    h_ref[...] = h
    hn = _rms_rows(h, g_ref[...])
    hn_hi = hn.astype(BF16)
    hn_ref[...] = hn_hi
    hn_lo = (hn - hn_hi.astype(F32)).astype(BF16)
    both = _dot(hn_hi, rw_ref[...])
    logits = both[:, :LANE] + both[:, LANE:] + _dot(hn_lo, rw_ref[:, :LANE])
    lt = logits.T[:ROUTE_ROWS]
    sub = lax.broadcasted_iota(I32, lt.shape, 0)
    subf = sub.astype(F32)
    neg = jnp.float32(-jnp.inf)
    lt = jnp.where(sub < n_experts, lt, neg)
    v1 = jnp.max(lt, axis=0, keepdims=True)
    i1 = jnp.min(jnp.where(lt == v1, subf, float(ROUTE_ROWS)), axis=0, keepdims=True)
    oh1 = subf == i1
    rest = jnp.where(oh1, neg, lt)
    v2 = jnp.max(rest, axis=0, keepdims=True)
    i2 = jnp.min(jnp.where(rest == v2, subf, float(ROUTE_ROWS)), axis=0, keepdims=True)
    oh2 = subf == i2
    e = jnp.exp(v2 - v1)
    w1 = 1.0 / (1.0 + e)
    w2 = e * w1
    cnt = jnp.where(oh1, 1.0, 0.0) + jnp.where(oh2, 1.0, 0.0)
    before = _dot(cnt.astype(BF16), triu_ref[...])
    counts = jnp.sum(cnt, axis=1, keepdims=True) + jnp.zeros((ROUTE_ROWS, LANE), F32)
    padded = jnp.floor((counts + (ROW_ALIGN - 1)) * (1.0 / ROW_ALIGN)) * ROW_ALIGN
    er = lax.broadcasted_iota(I32, (ROUTE_ROWS, ROUTE_ROWS), 0)
    ec = lax.broadcasted_iota(I32, (ROUTE_ROWS, ROUTE_ROWS), 1)
    seg = _dot(jnp.where(ec < er, 1.0, 0.0).astype(BF16), padded.astype(BF16))[:, 0:1]
    s1 = jnp.sum(jnp.where(oh1, before + seg, 0.0), axis=0, keepdims=True)
    s2 = jnp.sum(jnp.where(oh2, before + seg, 0.0), axis=0, keepdims=True)
    meta = jnp.where(sub == 0, i1, jnp.where(sub == 1, i2, jnp.where(sub == 2, w1, jnp.where(
        sub == 3, w2, jnp.where(sub == 4, s1, jnp.where(sub == 5, s2, 0.0))))))
    meta_ref[...] = meta[:8]
    cnt_ref[...] = counts


def _proj_router(a, w_bf16, res, gain, router):
    n, d = res.shape
    n_experts = router.shape[1]
    assert n_experts <= ROUTE_ROWS
    tm = min(TILES["tm_router"], n)
    rw = jnp.zeros((d, LANE), F32).at[:, :n_experts].set(router.astype(F32))
    rw_hi = rw.astype(BF16)
    rw_cat = jnp.concatenate([rw_hi, (rw - rw_hi.astype(F32)).astype(BF16)], axis=1)
    triu = jnp.triu(jnp.ones((tm, tm), F32), 1).astype(BF16)
    row = pl.BlockSpec((tm, d), lambda i: (i, 0))
    return pl.pallas_call(
        functools.partial(_proj_router_kernel, n_experts=n_experts),
        grid=(n // tm,),
        in_specs=[row, pl.BlockSpec((d, d), lambda i: (0, 0)), row,
                  pl.BlockSpec((1, d), lambda i: (0, 0)),
                  pl.BlockSpec((d, 2 * LANE), lambda i: (0, 0)),
                  pl.BlockSpec((tm, tm), lambda i: (0, 0))],
        out_specs=[row, row, pl.BlockSpec((8, tm), lambda i: (0, i)),
                   pl.BlockSpec((ROUTE_ROWS, LANE), lambda i: (i, 0))],
        out_shape=[jax.ShapeDtypeStruct((n, d), F32), jax.ShapeDtypeStruct((n, d), BF16),
                   jax.ShapeDtypeStruct((8, n), F32),
                   jax.ShapeDtypeStruct((n // tm * ROUTE_ROWS, LANE), F32)],
        compiler_params=_cparams(("arbitrary",)),
        name="proj_router",
    )(a, w_bf16, res, gain, rw_cat, triu)


def _for_each_piece(n_rows, max_rows, fn):
    size = ROW_ALIGN
    while size * 2 <= max_rows:
        size *= 2
    done = 0
    while size >= ROW_ALIGN:
        take = (n_rows & size) != 0

        @pl.when(take)
        def _(done=done, size=size):
            fn(done, size)

        done = done + jnp.where(take, size, 0)
        size //= 2


def _segment_copies(pc_ref, seg_ref, dst_ref, tile, n_experts, max_rows, make_copy, start):
    for e in range(n_experts):
        idx = tile * n_experts + e
        local0 = seg_ref[idx]
        sorted0 = dst_ref[idx]

        def piece(off, size, local0=local0, sorted0=sorted0):
            cp = make_copy(pl.multiple_of(local0 + off, ROW_ALIGN), pl.multiple_of(sorted0 + off, ROW_ALIGN), size)
            if start:
                cp.start()
            else:
                cp.wait()

        _for_each_piece(pc_ref[idx], max_rows, piece)


def _slot_one_hot(meta_ref, n_slots, gated):
    tokens = meta_ref.shape[1]
    sub = lax.broadcasted_iota(I32, (n_slots, tokens), 0)
    out = None
    for kk in range(TOP_K):
        slot = meta_ref[4 + kk:5 + kk, :].astype(I32)
        val = meta_ref[2 + kk:3 + kk, :] if gated else 1.0
        term = jnp.where(sub == slot, val, 0.0)
        out = term if out is None else out + term
    return out.astype(BF16)


ZERO_ROWS = 512


def _scatter_kernel(pc_ref, seg_ref, dst_ref, gap_ref, meta_ref, hn_ref, xs_ref, sorted_ref, zeros_ref, sem, zsem,
                    *, n_experts, n_slots, tmx, max_empty_tiles):
    tile = pl.program_id(0)
    last = pl.num_programs(0) - 1
    max_rows = hn_ref.shape[0]

    def copies(t, b, start):
        def make_copy(local, glob, size):
            return pltpu.make_async_copy(sorted_ref.at[b, pl.ds(local, size)], xs_ref.at[pl.ds(glob, size)],
                                         sem.at[b])
        _segment_copies(pc_ref, seg_ref, dst_ref, t, n_experts, max_rows, make_copy, start)

    def zero_fill(start):
        def go(row0, size):
            cp = pltpu.make_async_copy(zeros_ref.at[pl.ds(0, size)],
                                       xs_ref.at[pl.ds(pl.multiple_of(row0, ROW_ALIGN), size)], zsem)
            if start:
                cp.start()
            else:
                cp.wait()

        zrows = zeros_ref.shape[0]
        for e in range(n_experts):
            row0 = gap_ref[2 * e]
            _for_each_piece(gap_ref[2 * e + 1], zrows, lambda off, size, row0=row0: go(row0 + off, size))
        for t in range(max_empty_tiles):
            @pl.when(t < gap_ref[2 * n_experts + 1])
            def _(t=t):
                for r0 in range(0, tmx, zrows):
                    go(gap_ref[2 * n_experts] + t * tmx + r0, zrows)

    @pl.when(tile == 0)
    def _():
        zeros_ref[...] = jnp.zeros_like(zeros_ref)
        zero_fill(True)
        zero_fill(False)

    one_hot = _slot_one_hot(meta_ref, n_slots, False)
    for b in range(2):
        @pl.when(tile % 2 == b)
        def _(b=b):
            sorted_ref[b] = _dot(one_hot, hn_ref[...])
            copies(tile, b, True)

            @pl.when(tile > 0)
            def _():
                copies(tile - 1, 1 - b, False)

            @pl.when(tile == last)
            def _():
                copies(tile, b, False)


def _scatter_rows(hn, meta, pc, seg, dst, gaps, n_sorted, tm, tmx, n_experts):
    n, d = hn.shape
    n_slots = TOP_K * tm + ROW_ALIGN * n_experts
    assert tmx % ZERO_ROWS == 0 or tmx < ZERO_ROWS
    grid_spec = pltpu.PrefetchScalarGridSpec(
        num_scalar_prefetch=4,
        grid=(n // tm,),
        in_specs=[pl.BlockSpec((8, tm), lambda i, *_: (0, i)),
                  pl.BlockSpec((tm, d), lambda i, *_: (i, 0))],
        out_specs=pl.BlockSpec(memory_space=pl.ANY),
        scratch_shapes=[pltpu.VMEM((2, n_slots, d), F32), pltpu.VMEM((min(ZERO_ROWS, tmx), d), F32),
                        pltpu.SemaphoreType.DMA((2,)), pltpu.SemaphoreType.DMA(())],
    )
    return pl.pallas_call(
        functools.partial(_scatter_kernel, n_experts=n_experts, n_slots=n_slots, tmx=tmx,
                          max_empty_tiles=n_sorted // tmx - (TOP_K * n) // tmx),
        grid_spec=grid_spec,
        out_shape=jax.ShapeDtypeStruct((n_sorted, d), F32),
        compiler_params=pltpu.CompilerParams(dimension_semantics=("arbitrary",),
                                             vmem_limit_bytes=VMEM_LIMIT, has_side_effects=True),
        name="moe_scatter",
    )(pc, seg, dst, gaps, meta, hn)


EXPERT_SUB_ROWS = 512


def _expert_ffn_kernel(te_ref, na_ref, x_ref, wg_ref, wu_ref, wd_ref, y_ref, xb_ref, acc_ref):
    del te_ref
    i = pl.program_id(0)
    j = pl.program_id(1)

    @pl.when(i < na_ref[0])
    def _():
        @pl.when(j == 0)
        def _():
            xb_ref[...] = x_ref[...].astype(BF16)
            acc_ref[...] = jnp.zeros_like(acc_ref)

        sub_rows = min(EXPERT_SUB_ROWS, xb_ref.shape[0])
        for r0 in range(0, xb_ref.shape[0], sub_rows):
            rows = slice(r0, r0 + sub_rows)
            xb = xb_ref[rows, :]
            act = (_silu(_dot(xb, wg_ref[...])) * _dot(xb, wu_ref[...])).astype(BF16)
            acc_ref[rows, :] += _dot(act, wd_ref[...])

        @pl.when(j == pl.num_programs(1) - 1)
        def _():
            y_ref[...] = acc_ref[...]

    @pl.when((i >= na_ref[0]) & (j == pl.num_programs(1) - 1))
    def _():
        y_ref[...] = jnp.zeros_like(y_ref)


def _expert_ffn(xs, tile_expert, n_active, w_gate_up_bf16, w_down_bf16, tmx):
    n_sorted, d = xs.shape
    f = w_down_bf16.shape[1]
    fb = _hidden_block(f, TILES["fbx"])
    nf = f // fb
    n_tiles = n_sorted // tmx

    def row_map(i, j, te, na):
        return (jnp.minimum(i, na[0] - 1), 0)

    def col(i, j, na):
        return jnp.where(i < na[0], j, nf - 1)

    grid_spec = pltpu.PrefetchScalarGridSpec(
        num_scalar_prefetch=2,
        grid=(n_tiles, nf),
        in_specs=[
            pl.BlockSpec((tmx, d), row_map),
            pl.BlockSpec((None, d, fb), lambda i, j, te, na: (te[i], 0, col(i, j, na))),
            pl.BlockSpec((None, d, fb), lambda i, j, te, na: (te[i], 0, col(i, j, na) + nf)),
            pl.BlockSpec((None, fb, d), lambda i, j, te, na: (te[i], col(i, j, na), 0)),
        ],
        out_specs=pl.BlockSpec((tmx, d), lambda i, j, te, na: (i, 0)),
        scratch_shapes=[pltpu.VMEM((tmx, d), BF16), pltpu.VMEM((tmx, d), F32)],
    )
    return pl.pallas_call(
        _expert_ffn_kernel,
        grid_spec=grid_spec,
        out_shape=jax.ShapeDtypeStruct((n_sorted, d), F32),
        compiler_params=_cparams(("arbitrary", "arbitrary")),
        name="expert_ffn",
    )(tile_expert, n_active, xs, w_gate_up_bf16, w_gate_up_bf16, w_down_bf16)


def _combine_kernel(pc_ref, seg_ref, dst_ref, meta_ref, h_ref, ys_ref, o_ref, buf_ref, sem,
                    *, n_experts, n_slots):
    tile = pl.program_id(0)
    last = pl.num_programs(0) - 1
    max_rows = h_ref.shape[0]

    def copies(t, b, start):
        def make_copy(local, glob, size):
            return pltpu.make_async_copy(ys_ref.at[pl.ds(glob, size)], buf_ref.at[b, pl.ds(local, size)],
                                         sem.at[b])
        _segment_copies(pc_ref, seg_ref, dst_ref, t, n_experts, max_rows, make_copy, start)

    @pl.when(tile == 0)
    def _():
        buf_ref[...] = jnp.zeros_like(buf_ref)
        copies(0, 0, True)

    gates = _slot_one_hot(meta_ref, n_slots, True)
    for b in range(2):
        @pl.when(tile % 2 == b)
        def _(b=b):
            @pl.when(tile < last)
            def _():
                copies(tile + 1, 1 - b, True)

            copies(tile, b, False)
            o_ref[...] = h_ref[...] + _dot_tn(gates, buf_ref[b].astype(BF16))


def _combine(h, meta, pc, seg, dst, ys, tm, n_experts):
    n, d = h.shape
    n_slots = TOP_K * tm + ROW_ALIGN * n_experts
    grid_spec = pltpu.PrefetchScalarGridSpec(
        num_scalar_prefetch=3,
        grid=(n // tm,),
        in_specs=[pl.BlockSpec((8, tm), lambda i, *_: (0, i)),
                  pl.BlockSpec((tm, d), lambda i, *_: (i, 0)),
                  pl.BlockSpec(memory_space=pl.ANY)],
        out_specs=pl.BlockSpec((tm, d), lambda i, *_: (i, 0)),
        scratch_shapes=[pltpu.VMEM((2, n_slots, d), F32), pltpu.SemaphoreType.DMA((2,))],
    )
    return pl.pallas_call(
        functools.partial(_combine_kernel, n_experts=n_experts, n_slots=n_slots),
        grid_spec=grid_spec,
        out_shape=jax.ShapeDtypeStruct((n, d), F32),
        compiler_params=_cparams(("arbitrary",)),
        name="moe_combine",
    )(pc, seg, dst, meta, h, ys)


def _row(vec):
    return vec.astype(F32).reshape(1, -1)


def _round_up(x, m):
    return (x + m - 1) // m * m


def kernel(x, lower_bounds, l0_mix_norm, l0_hgrn_w_in, l0_hgrn_out_norm, l0_hgrn_w_out, l0_ffn_norm, l0_ffn_w_gate_up, l0_ffn_w_down, l1_mix_norm, l1_diff_w_in, l1_q_norm, l1_k_norm, l1_lambda_q1, l1_lambda_k1, l1_lambda_q2, l1_lambda_k2, l1_diff_sub_norm, l1_diff_w_out, l1_ffn_norm, l1_router, l1_moe_w_gate_up, l1_moe_w_down):
    batch, seq, d = x.shape
    n = batch * seq
    heads = d // LANE
    n_experts = l1_router.shape[1]
    x2 = x.reshape(n, d).astype(F32)

    lb0 = jnp.cumsum(jax.nn.softmax(lower_bounds.astype(F32), axis=0), axis=0)[0]
    q, k, lf, v, gs = _hgrn_in(x2, _row(l0_mix_norm), _row(lb0), l0_hgrn_w_in.astype(BF16))
    og = _hgrn_rec(q, k, lf, v, gs, _row(l0_hgrn_out_norm), batch, seq)
    h = _ffn(og, l0_hgrn_w_out.astype(BF16), x2, _row(l0_ffn_norm),
             l0_ffn_w_gate_up.astype(BF16), l0_ffn_w_down.astype(BF16))

    lam_init = 0.8 - 0.6 * math.exp(-0.3 * 1)
    q, k, vt = _diff_in(h, _row(l1_mix_norm), l1_diff_w_in.astype(BF16),
                        _row(jnp.tile(l1_q_norm, 2 * heads)), _row(jnp.tile(l1_k_norm, 2 * heads)), batch, seq)
    slopes = 2.0 ** (-8.0 * jnp.arange(1, heads + 1, dtype=F32) / heads)
    lam_rows = jnp.zeros((8, LANE), F32).at[:4, :DIFF_HEAD_DIM].set(
        jnp.stack([l1_lambda_q1, l1_lambda_k1, l1_lambda_q2, l1_lambda_k2]).astype(F32))
    oa = _attention(q, k, vt, slopes, lam_rows, _row(l1_diff_sub_norm), batch, seq, lam_init)

    h, hn, meta, tile_cnt = _proj_router(oa, l1_diff_w_out.astype(BF16), h, _row(l1_ffn_norm), l1_router)
    tm = min(TILES["tm_router"], n)
    n_tok_tiles = n // tm
    tmx = min(TILES["tmx"], n)
    cnt = tile_cnt.reshape(n_tok_tiles, ROUTE_ROWS, LANE)[:, :n_experts, 0].astype(I32)
    pc = _round_up(cnt, ROW_ALIGN)
    seg = jnp.cumsum(pc, axis=1) - pc
    region = _round_up(jnp.sum(pc, axis=0), tmx)
    region_end = jnp.cumsum(region)
    dst = (region_end - region)[None, :] + jnp.cumsum(pc, axis=0) - pc
    n_x_tiles = (TOP_K * n + n_tok_tiles * n_experts * (ROW_ALIGN - 1)) // tmx + n_experts
    n_active = (region_end[-1:] // tmx).astype(I32)
    x_tile_start = jnp.arange(n_x_tiles, dtype=I32) * tmx
    tile_expert = jnp.sum((x_tile_start[:, None] >= region_end[None, :]).astype(I32), axis=1)
    tile_expert = jnp.minimum(tile_expert, tile_expert[n_active[0] - 1]).astype(I32)
    pc_f, seg_f, dst_f = (t.reshape(-1).astype(I32) for t in (pc, seg, dst))
    used = jnp.sum(pc, axis=0)
    gaps = jnp.concatenate([jnp.stack([region_end - region + used, region - used], axis=1).reshape(-1),
                            region_end[-1:], n_x_tiles - n_active]).astype(I32)

    xs = _scatter_rows(hn, meta, pc_f, seg_f, dst_f, gaps, n_x_tiles * tmx, tm, tmx, n_experts)
    ys = _expert_ffn(xs, tile_expert, n_active, l1_moe_w_gate_up.astype(BF16),
                     l1_moe_w_down.astype(BF16), tmx)
    out = _combine(h, meta, pc_f, seg_f, dst_f, ys, tm, n_experts)
    return out.reshape(batch, seq, d).astype(x.dtype)
```

```python
import functools
import math

import jax
import jax.numpy as jnp
from jax import lax
from jax.experimental import pallas as pl
from jax.experimental.pallas import tpu as pltpu

F32 = jnp.float32
BF16 = jnp.bfloat16
I32 = jnp.int32

NORM_EPS = 1e-6
LANE = 128
HGRN_CHUNK = 64
DIFF_HEAD_DIM = 64
TOP_K = 2
LOG2E = math.log2(math.e)
EXP_CLAMP = 60.0
VMEM_LIMIT = 56 * 1024 * 1024

TILES = dict(
    tm_in=512,
    tb_rec=512,
    tm_ffn=512,
    tq=1024,
    tm_router=512,
    tmx=1024,
    fbx=512,
)


def _cparams(sem):
    return pltpu.CompilerParams(dimension_semantics=sem, vmem_limit_bytes=VMEM_LIMIT)


def _sigmoid(x):
    return 1.0 / (1.0 + jnp.exp(-x))


def _silu(x):
    return x * _sigmoid(x)


def _rms_rows(x, gain):
    ms = jnp.mean(x * x, axis=-1, keepdims=True)
    return x * lax.rsqrt(ms + NORM_EPS) * gain


def _dot(a, b):
    return jnp.dot(a, b, preferred_element_type=F32)


def _dot_nt(a, b):
    return lax.dot_general(a, b, (((1,), (1,)), ((), ())), preferred_element_type=F32)


def _dot_tn(a, b):
    return lax.dot_general(a, b, (((0,), (0,)), ((), ())), preferred_element_type=F32)


def _hgrn_in_kernel(x_ref, g_ref, lb_ref, w_ref, q_ref, k_ref, lf_ref, v_ref, gs_ref):
    d = x_ref.shape[1]
    xn = _rms_rows(x_ref[...], g_ref[...]).astype(BF16)

    def proj(j):
        return _dot(xn, w_ref[:, j * d:(j + 1) * d])

    q_ref[...] = _silu(proj(0)).astype(BF16)
    lb = lb_ref[...]
    forget = lb + (1.0 - lb) * _sigmoid(proj(1))
    k_ref[...] = (1.0 - forget).astype(BF16)
    lf_ref[...] = jnp.log(forget)
    gs_ref[...] = _silu(proj(3)).astype(BF16)
    v_ref[...] = proj(2).astype(BF16)


def _hgrn_in(x, gain, lb, w_bf16):
    n, d = x.shape
    tm = min(TILES["tm_in"], n)
    row = pl.BlockSpec((tm, d), lambda i: (i, 0))
    vec = pl.BlockSpec((1, d), lambda i: (0, 0))
    return pl.pallas_call(
        _hgrn_in_kernel,
        grid=(n // tm,),
        in_specs=[row, vec, vec, pl.BlockSpec((d, 4 * d), lambda i: (0, 0))],
        out_specs=[row, row, row, row, row],
        out_shape=[
            jax.ShapeDtypeStruct((n, d), BF16),
            jax.ShapeDtypeStruct((n, d), BF16),
            jax.ShapeDtypeStruct((n, d), F32),
            jax.ShapeDtypeStruct((n, d), BF16),
            jax.ShapeDtypeStruct((n, d), BF16),
        ],
        compiler_params=_cparams(("arbitrary",)),
        name="hgrn_in",
    )(x, gain, lb, w_bf16)


def _hgrn_rec_kernel(q_ref, k_ref, lf_ref, v_ref, gs_ref, on_ref, o_ref, st_ref, b_ref, qt_ref, kt_ref,
                     *, chunk, heads):
    @pl.when(pl.program_id(1) == 0)
    def _():
        st_ref[...] = jnp.zeros_like(st_ref)

    c = chunk
    tb, d = q_ref.shape
    n_chunks = tb // c
    row = lax.broadcasted_iota(I32, (c, c), 0)
    col = lax.broadcasted_iota(I32, (c, c), 1)
    causal = col <= row
    mid = c // 2 - 1

    brow = lax.broadcasted_iota(I32, (tb, tb), 0)
    bcol = lax.broadcasted_iota(I32, (tb, tb), 1)
    tri = jnp.where((bcol <= brow) & (bcol >= brow - (brow & (c - 1))), 1.0, 0.0).astype(BF16)
    lf = lf_ref[...]
    hi = lf.astype(BF16)
    lo = (lf - hi.astype(F32)).astype(BF16)
    b = _dot(tri, hi) + _dot(tri, lo)
    b_ref[...] = b
    b3 = b.reshape(n_chunks, c, d)
    bmid3 = b3[:, mid:mid + 1, :]
    qt_ref[...] = (q_ref[...].astype(F32)
                   * jnp.exp(jnp.minimum(b3 - bmid3, EXP_CLAMP)).reshape(tb, d)).astype(BF16)
    kt_ref[...] = (k_ref[...].astype(F32)
                   * jnp.exp(jnp.minimum(bmid3 - b3, EXP_CLAMP)).reshape(tb, d)).astype(BF16)

    def body(ci, carry):
        r0 = pl.multiple_of(ci * c, c)
        rows = pl.ds(r0, c)
        qt = qt_ref[rows, :]
        kt = kt_ref[rows, :]
        v = v_ref[rows, :]
        gs = gs_ref[rows, :].astype(F32)
        bmid = b_ref[pl.ds(r0 + mid, 1), :]
        bend = b_ref[pl.ds(r0 + c - 1, 1), :]
        e_mid = jnp.exp(bmid)
        e_end = jnp.exp(bend)
        e_end_mid = jnp.exp(bend - bmid)
        for h in range(heads):
            sl = slice(h * LANE, (h + 1) * LANE)
            qh, kh, vh = qt[:, sl], kt[:, sl], v[:, sl]
            a = jnp.where(causal, _dot_nt(qh, kh), 0.0).astype(BF16)
            st = st_ref[h]
            o = _dot(a, vh) + _dot_nt(qh, (st * e_mid[:, sl]).astype(BF16))
            st_ref[h] = st * e_end[:, sl] + _dot_tn(vh, kh) * e_end_mid[:, sl]
            og = _rms_rows(o, on_ref[:, sl]) * gs[:, sl]
            o_ref[rows, sl] = og.astype(BF16)
        return carry

    lax.fori_loop(0, n_chunks, body, 0, unroll=True)


def _hgrn_rec(q, k, lf, v, gs, out_norm, batch, seq):
    n, d = q.shape
    heads = d // LANE
    tb = min(TILES["tb_rec"], seq)
    nt = seq // tb
    blk = pl.BlockSpec((tb, d), lambda b, t: (b * nt + t, 0))
    return pl.pallas_call(
        functools.partial(_hgrn_rec_kernel, chunk=HGRN_CHUNK, heads=heads),
        grid=(batch, nt),
        in_specs=[blk, blk, blk, blk, blk, pl.BlockSpec((1, d), lambda b, t: (0, 0))],
        out_specs=blk,
        out_shape=jax.ShapeDtypeStruct((n, d), BF16),
        scratch_shapes=[pltpu.VMEM((heads, LANE, LANE), F32), pltpu.VMEM((tb, d), F32),
                        pltpu.VMEM((tb, d), BF16), pltpu.VMEM((tb, d), BF16)],
        compiler_params=_cparams(("arbitrary", "arbitrary")),
        name="hgrn_rec",
    )(q, k, lf, v, gs, out_norm)


def _ffn_kernel(a_ref, wo_ref, x_ref, g_ref, wgu_ref, wd_ref, o_ref):
    f = wd_ref.shape[0]
    h = x_ref[...] + _dot(a_ref[...], wo_ref[...])
    xn = _rms_rows(h, g_ref[...]).astype(BF16)
    gu = _dot(xn, wgu_ref[...])
    act = (_silu(gu[:, :f]) * gu[:, f:]).astype(BF16)
    o_ref[...] = h + _dot(act, wd_ref[...])


def _hidden_block(f, want):
    best = None
    for fb in range(LANE, f + 1, LANE):
        if f % fb == 0 and fb <= want:
            best = fb
    assert best is not None, f
    return best


def _ffn(mix, w_out_bf16, x, gain, w_gate_up_bf16, w_down_bf16):
    n, d = x.shape
    f = w_down_bf16.shape[0]
    tm = min(TILES["tm_ffn"], n)
    row = pl.BlockSpec((tm, d), lambda i: (i, 0))
    once = pl.Buffered(1)
    return pl.pallas_call(
        _ffn_kernel,
        grid=(n // tm,),
        in_specs=[
            row,
            pl.BlockSpec((d, d), lambda i: (0, 0), pipeline_mode=once),
            row,
            pl.BlockSpec((1, d), lambda i: (0, 0)),
            pl.BlockSpec((d, 2 * f), lambda i: (0, 0), pipeline_mode=once),
            pl.BlockSpec((f, d), lambda i: (0, 0), pipeline_mode=once),
        ],
        out_specs=row,
        out_shape=jax.ShapeDtypeStruct((n, d), F32),
        compiler_params=_cparams(("arbitrary",)),
        name="ffn",
    )(mix, w_out_bf16, x, gain, w_gate_up_bf16, w_down_bf16)


def _group_rms(y, gmat, gain):
    d = y.shape[1]
    sq = (y * y).astype(BF16)
    ms = jnp.concatenate(
        [_dot(sq[:, s * LANE:(s + 1) * LANE], gmat) for s in range(d // LANE)], axis=1)
    return y * lax.rsqrt(ms + NORM_EPS) * gain


V_ROWS = LANE + 16


def _diff_in_kernel(x_ref, g_ref, w_ref, gm_ref, qn_ref, kn_ref, q_ref, k_ref, vt_ref):
    d = x_ref.shape[1]
    xn = _rms_rows(x_ref[...], g_ref[...]).astype(BF16)

    def proj(j):
        return _dot(xn, w_ref[:, j * d:(j + 1) * d])

    q_ref[...] = (_group_rms(proj(0), gm_ref[...], qn_ref[...]) * (DIFF_HEAD_DIM ** -0.5 * LOG2E)).astype(BF16)
    k_ref[...] = _group_rms(proj(1), gm_ref[...], kn_ref[...]).astype(BF16)
    v = proj(2)
    for h in range(d // LANE):
        vt_ref[h, :LANE, :] = v[:, h * LANE:(h + 1) * LANE].T.astype(BF16)
        vt_ref[h, LANE:, :] = jnp.ones((V_ROWS - LANE, v.shape[0]), BF16)


def _diff_in(x, gain, w_bf16, q_norm_row, k_norm_row, batch, seq):
    n, d = x.shape
    heads = d // LANE
    tm = min(TILES["tm_in"], seq)
    nt = seq // tm
    lane_group = jnp.arange(LANE) // DIFF_HEAD_DIM
    gmat = ((lane_group[:, None] == lane_group[None, :]).astype(F32) / DIFF_HEAD_DIM).astype(BF16)
    row = pl.BlockSpec((tm, d), lambda i: (i, 0))
    vec = pl.BlockSpec((1, d), lambda i: (0, 0))
    return pl.pallas_call(
        _diff_in_kernel,
        grid=(n // tm,),
        in_specs=[row, vec, pl.BlockSpec((d, 3 * d), lambda i: (0, 0)),
                  pl.BlockSpec((LANE, LANE), lambda i: (0, 0)), vec, vec],
        out_specs=[row, row, pl.BlockSpec((None, heads, V_ROWS, tm), lambda i: (i // nt, 0, 0, i % nt))],
        out_shape=[jax.ShapeDtypeStruct((n, d), BF16), jax.ShapeDtypeStruct((n, d), BF16),
                   jax.ShapeDtypeStruct((batch, heads, V_ROWS, seq), BF16)],
        compiler_params=_cparams(("arbitrary",)),
        name="diff_in",
    )(x, gain, w_bf16, gmat, q_norm_row, k_norm_row)


ATTN_UNROLL = 4
FULL, DIAG_A, DIAG_B = range(3)


def _attn_kernel(slope_ref, q_ref, k_ref, vt_ref, lam_ref, sn_ref, o_ref, m_ref, acc_ref, w_ref,
                 *, tq, lam_init):
    h = pl.program_id(1)
    qb = pl.program_id(2)
    slope = slope_ref[h] * LOG2E

    q = q_ref[...]
    lane = lax.broadcasted_iota(I32, q.shape, 1)
    zero = jnp.zeros_like(q)
    qs = jnp.concatenate([jnp.where(lane < DIFF_HEAD_DIM, q, zero),
                          jnp.where(lane >= DIFF_HEAD_DIM, q, zero)], axis=0)

    nq2 = 2 * tq
    m_ref[...] = jnp.full_like(m_ref, -jnp.inf)
    acc_ref[...] = jnp.zeros_like(acc_ref)

    lane_k = lax.broadcasted_iota(I32, (tq, LANE), 1)
    bias = slope * lax.broadcasted_iota(I32, (tq, LANE), 0).astype(F32)
    bias_hi = bias.astype(BF16).astype(F32)
    bias_cols = jnp.where(lane_k == 0, bias_hi, jnp.where(lane_k == 1, bias - bias_hi, 0.0)).astype(BF16)
    lane_q = lax.broadcasted_iota(I32, (2 * tq, LANE), 1)
    qsx = jnp.concatenate([qs, jnp.where(lane_q < 2, 1.0, 0.0).astype(BF16)], axis=1)

    half = tq // 2
    qsx_b = jnp.concatenate([qsx[half:tq], qsx[tq + half:]], axis=0)

    def put(buf, kind, kb):
        k0 = pl.multiple_of(kb * tq, tq)
        if kind == FULL:
            w_ref[buf, :, :nq2] = _dot_nt(jnp.concatenate([k_ref[pl.ds(k0, tq), :], bias_cols], axis=1), qsx)
        elif kind == DIAG_A:
            kx = jnp.concatenate([k_ref[pl.ds(k0, half), :], bias_cols[:half]], axis=1)
            w_ref[buf, :half, :nq2] = _dot_nt(kx, qsx)
        else:
            kx = jnp.concatenate([k_ref[pl.ds(k0 + half, half), :], bias_cols[half:]], axis=1)
            w_ref[buf, :half, :tq] = _dot_nt(kx, qsx_b)

    def consume(buf, kind, kb):
        k0 = pl.multiple_of(kb * tq, tq)
        if kind == FULL:
            w = w_ref[buf, :, :nq2]
            keys = pl.ds(k0, tq)
            off = slope * ((kb - qb) * tq).astype(F32)
        else:
            period = tq if kind == DIAG_A else half
            w = w_ref[buf, :half, :nq2] if kind == DIAG_A else w_ref[buf, :half, :tq]
            keys = pl.ds(k0, half) if kind == DIAG_A else pl.ds(k0 + half, half)
            r = lax.broadcasted_iota(I32, w.shape, 0)
            c = lax.broadcasted_iota(I32, w.shape, 1)
            w = jnp.where(r <= jnp.where(c >= period, c - period, c), w, -jnp.inf)
            off = 0.0
        if kind == DIAG_B:
            m_old = jnp.concatenate([m_ref[:, half:tq], m_ref[:, tq + half:]], axis=1)
        else:
            m_old = m_ref[...]
        m_new = jnp.maximum(m_old, jnp.max(w, axis=0, keepdims=True) + off)
        alpha = jnp.exp2(m_old - m_new)
        p = jnp.exp2(w - (m_new - off)).astype(BF16)
        pv = _dot(vt_ref[:, keys], p)
        if kind == DIAG_B:
            acc_ref[:, half:tq] = alpha[:, :half] * acc_ref[:, half:tq] + pv[:, :half]
            acc_ref[:, tq + half:nq2] = alpha[:, half:] * acc_ref[:, tq + half:nq2] + pv[:, half:]
            m_ref[:, half:tq] = m_new[:, :half]
            m_ref[:, tq + half:] = m_new[:, half:]
        else:
            acc_ref[:, :nq2] = alpha * acc_ref[:, :nq2] + pv
            m_ref[...] = m_new

    put(0, DIAG_B, qb)

    @pl.when(qb == 0)
    def _():
        put(1, DIAG_A, qb)
        consume(0, DIAG_B, qb)
        consume(1, DIAG_A, qb)

    @pl.when(qb > 0)
    def _():
        put(1, FULL, 0)
        consume(0, DIAG_B, qb)

    def group(g, carry):
        first = ATTN_UNROLL * g
        for u in range(ATTN_UNROLL):
            put(u % 2, FULL, first + u + 1)
            consume((u + 1) % 2, FULL, first + u)
        return carry

    n_groups = jnp.maximum(qb - 1, 0) // ATTN_UNROLL
    lax.fori_loop(0, n_groups, group, 0)
    rest = qb - ATTN_UNROLL * n_groups
    for r in range(1, ATTN_UNROLL + 1):
        @pl.when(rest == r)
        def _(r=r):
            for u in range(r):
                if u + 1 < r:
                    put(u % 2, FULL, qb - r + u + 1)
                else:
                    put(u % 2, DIAG_A, qb)
                consume((u + 1) % 2, FULL, qb - r + u)
            consume((r + 1) % 2, DIAG_A, qb)

    lam_rows = lam_ref[...]
    lam = (jnp.exp(jnp.sum(lam_rows[0:1] * lam_rows[1:2], axis=1, keepdims=True))
           - jnp.exp(jnp.sum(lam_rows[2:3] * lam_rows[3:4], axis=1, keepdims=True)) + lam_init)
    acc = acc_ref[:, :nq2]
    on = acc[:LANE] / acc[LANE:LANE + 1]
    o = (on[:, :tq] - lam * on[:, tq:]).T
    o_ref[...] = (_rms_rows(o, sn_ref[...]) * (1.0 - lam_init)).astype(BF16)


def _attention(q, k, vt, slopes, lam_rows, sub_norm_row, batch, seq, lam_init):
    n, d = q.shape
    heads = d // LANE
    tq = min(TILES["tq"], seq)
    nq = seq // tq
    grid_spec = pltpu.PrefetchScalarGridSpec(
        num_scalar_prefetch=1,
        grid=(batch, heads, nq),
        in_specs=[
            pl.BlockSpec((tq, LANE), lambda b, h, i, s: (b * nq + i, h)),
            pl.BlockSpec((seq, LANE), lambda b, h, i, s: (b, h)),
            pl.BlockSpec((None, None, V_ROWS, seq), lambda b, h, i, s: (b, h, 0, 0)),
            pl.BlockSpec((8, LANE), lambda b, h, i, s: (0, 0)),
            pl.BlockSpec((1, LANE), lambda b, h, i, s: (0, 0)),
        ],
        out_specs=pl.BlockSpec((tq, LANE), lambda b, h, i, s: (b * nq + i, h)),
        scratch_shapes=[pltpu.VMEM((1, 2 * tq), F32), pltpu.VMEM((V_ROWS, 2 * tq + LANE), F32),
                        pltpu.VMEM((2, tq, 2 * tq + LANE), F32)],
    )
    return pl.pallas_call(
        functools.partial(_attn_kernel, tq=tq, lam_init=lam_init),
        grid_spec=grid_spec,
        out_shape=jax.ShapeDtypeStruct((n, d), BF16),
        compiler_params=_cparams(("arbitrary", "arbitrary", "arbitrary")),
        name="diff_attn",
    )(slopes, q, k, vt, lam_rows, sub_norm_row)


ROUTE_ROWS = 16
ROW_ALIGN = 8


def _proj_router_kernel(a_ref, w_ref, r_ref, g_ref, rw_ref, triu_ref, h_ref, hn_ref, meta_ref, cnt_ref,
                        *, n_experts):
    h = r_ref[...] + _dot(a_ref[...], w_ref[...])
    h_ref[...] = h
    hn = _rms_rows(h, g_ref[...])
    hn_hi = hn.astype(BF16)
    hn_ref[...] = hn_hi
    hn_lo = (hn - hn_hi.astype(F32)).astype(BF16)
    both = _dot(hn_hi, rw_ref[...])
    logits = both[:, :LANE] + both[:, LANE:] + _dot(hn_lo, rw_ref[:, :LANE])
    lt = logits.T[:ROUTE_ROWS]
    sub = lax.broadcasted_iota(I32, lt.shape, 0)
    subf = sub.astype(F32)
    neg = jnp.float32(-jnp.inf)
    lt = jnp.where(sub < n_experts, lt, neg)
    v1 = jnp.max(lt, axis=0, keepdims=True)
    i1 = jnp.min(jnp.where(lt == v1, subf, float(ROUTE_ROWS)), axis=0, keepdims=True)
    oh1 = subf == i1
    rest = jnp.where(oh1, neg, lt)
    v2 = jnp.max(rest, axis=0, keepdims=True)
    i2 = jnp.min(jnp.where(rest == v2, subf, float(ROUTE_ROWS)), axis=0, keepdims=True)
    oh2 = subf == i2
    e = jnp.exp(v2 - v1)
    w1 = 1.0 / (1.0 + e)
    w2 = e * w1
    cnt = jnp.where(oh1, 1.0, 0.0) + jnp.where(oh2, 1.0, 0.0)
    before = _dot(cnt.astype(BF16), triu_ref[...])
    counts = jnp.sum(cnt, axis=1, keepdims=True) + jnp.zeros((ROUTE_ROWS, LANE), F32)
    padded = jnp.floor((counts + (ROW_ALIGN - 1)) * (1.0 / ROW_ALIGN)) * ROW_ALIGN
    er = lax.broadcasted_iota(I32, (ROUTE_ROWS, ROUTE_ROWS), 0)
    ec = lax.broadcasted_iota(I32, (ROUTE_ROWS, ROUTE_ROWS), 1)
    seg = _dot(jnp.where(ec < er, 1.0, 0.0).astype(BF16), padded.astype(BF16))[:, 0:1]
    s1 = jnp.sum(jnp.where(oh1, before + seg, 0.0), axis=0, keepdims=True)
    s2 = jnp.sum(jnp.where(oh2, before + seg, 0.0), axis=0, keepdims=True)
    meta = jnp.where(sub == 0, i1, jnp.where(sub == 1, i2, jnp.where(sub == 2, w1, jnp.where(
        sub == 3, w2, jnp.where(sub == 4, s1, jnp.where(sub == 5, s2, 0.0))))))
    meta_ref[...] = meta[:8]
    cnt_ref[...] = counts


def _proj_router(a, w_bf16, res, gain, router):
    n, d = res.shape
    n_experts = router.shape[1]
    assert n_experts <= ROUTE_ROWS
    tm = min(TILES["tm_router"], n)
    rw = jnp.zeros((d, LANE), F32).at[:, :n_experts].set(router.astype(F32))
    rw_hi = rw.astype(BF16)
    rw_cat = jnp.concatenate([rw_hi, (rw - rw_hi.astype(F32)).astype(BF16)], axis=1)
    triu = jnp.triu(jnp.ones((tm, tm), F32), 1).astype(BF16)
    row = pl.BlockSpec((tm, d), lambda i: (i, 0))
    return pl.pallas_call(
        functools.partial(_proj_router_kernel, n_experts=n_experts),
        grid=(n // tm,),
        in_specs=[row, pl.BlockSpec((d, d), lambda i: (0, 0)), row,
                  pl.BlockSpec((1, d), lambda i: (0, 0)),
                  pl.BlockSpec((d, 2 * LANE), lambda i: (0, 0)),
                  pl.BlockSpec((tm, tm), lambda i: (0, 0))],
        out_specs=[row, row, pl.BlockSpec((8, tm), lambda i: (0, i)),
                   pl.BlockSpec((ROUTE_ROWS, LANE), lambda i: (i, 0))],
        out_shape=[jax.ShapeDtypeStruct((n, d), F32), jax.ShapeDtypeStruct((n, d), BF16),
                   jax.ShapeDtypeStruct((8, n), F32),
                   jax.ShapeDtypeStruct((n // tm * ROUTE_ROWS, LANE), F32)],
        compiler_params=_cparams(("arbitrary",)),
        name="proj_router",
    )(a, w_bf16, res, gain, rw_cat, triu)


def _for_each_piece(n_rows, max_rows, fn):
    size = ROW_ALIGN
    while size * 2 <= max_rows:
        size *= 2
    done = 0
    while size >= ROW_ALIGN:
        take = (n_rows & size) != 0

        @pl.when(take)
        def _(done=done, size=size):
            fn(done, size)

        done = done + jnp.where(take, size, 0)
        size //= 2


def _segment_copies(pc_ref, seg_ref, dst_ref, tile, n_experts, max_rows, make_copy, start):
    for e in range(n_experts):
        idx = tile * n_experts + e
        local0 = seg_ref[idx]
        sorted0 = dst_ref[idx]

        def piece(off, size, local0=local0, sorted0=sorted0):
            cp = make_copy(pl.multiple_of(local0 + off, ROW_ALIGN), pl.multiple_of(sorted0 + off, ROW_ALIGN), size)
            if start:
                cp.start()
            else:
                cp.wait()

        _for_each_piece(pc_ref[idx], max_rows, piece)


def _slot_one_hot(meta_ref, n_slots, gated):
    tokens = meta_ref.shape[1]
    sub = lax.broadcasted_iota(I32, (n_slots, tokens), 0)
    out = None
    for kk in range(TOP_K):
        slot = meta_ref[4 + kk:5 + kk, :].astype(I32)
        val = meta_ref[2 + kk:3 + kk, :] if gated else 1.0
        term = jnp.where(sub == slot, val, 0.0)
        out = term if out is None else out + term
    return out.astype(BF16)


ZERO_ROWS = 512


def _scatter_kernel(pc_ref, seg_ref, dst_ref, gap_ref, meta_ref, hn_ref, xs_ref, sorted_ref, zeros_ref, sem, zsem,
                    *, n_experts, n_slots, tmx, max_empty_tiles):
    tile = pl.program_id(0)
    last = pl.num_programs(0) - 1
    max_rows = hn_ref.shape[0]

    def copies(t, b, start):
        def make_copy(local, glob, size):
            return pltpu.make_async_copy(sorted_ref.at[b, pl.ds(local, size)], xs_ref.at[pl.ds(glob, size)],
                                         sem.at[b])
        _segment_copies(pc_ref, seg_ref, dst_ref, t, n_experts, max_rows, make_copy, start)

    def zero_fill(start):
        def go(row0, size):
            cp = pltpu.make_async_copy(zeros_ref.at[pl.ds(0, size)],
                                       xs_ref.at[pl.ds(pl.multiple_of(row0, ROW_ALIGN), size)], zsem)
            if start:
                cp.start()
            else:
                cp.wait()

        zrows = zeros_ref.shape[0]
        for e in range(n_experts):
            row0 = gap_ref[2 * e]
            _for_each_piece(gap_ref[2 * e + 1], zrows, lambda off, size, row0=row0: go(row0 + off, size))
        for t in range(max_empty_tiles):
            @pl.when(t < gap_ref[2 * n_experts + 1])
            def _(t=t):
                for r0 in range(0, tmx, zrows):
                    go(gap_ref[2 * n_experts] + t * tmx + r0, zrows)

    @pl.when(tile == 0)
    def _():
        zeros_ref[...] = jnp.zeros_like(zeros_ref)
        zero_fill(True)
        zero_fill(False)

    one_hot = _slot_one_hot(meta_ref, n_slots, False)
    for b in range(2):
        @pl.when(tile % 2 == b)
        def _(b=b):
            sorted_ref[b] = _dot(one_hot, hn_ref[...])
            copies(tile, b, True)

            @pl.when(tile > 0)
            def _():
                copies(tile - 1, 1 - b, False)

            @pl.when(tile == last)
            def _():
                copies(tile, b, False)


def _scatter_rows(hn, meta, pc, seg, dst, gaps, n_sorted, tm, tmx, n_experts):
    n, d = hn.shape
    n_slots = TOP_K * tm + ROW_ALIGN * n_experts
    assert tmx % ZERO_ROWS == 0 or tmx < ZERO_ROWS
    grid_spec = pltpu.PrefetchScalarGridSpec(
        num_scalar_prefetch=4,
        grid=(n // tm,),
        in_specs=[pl.BlockSpec((8, tm), lambda i, *_: (0, i)),
                  pl.BlockSpec((tm, d), lambda i, *_: (i, 0))],
        out_specs=pl.BlockSpec(memory_space=pl.ANY),
        scratch_shapes=[pltpu.VMEM((2, n_slots, d), F32), pltpu.VMEM((min(ZERO_ROWS, tmx), d), F32),
                        pltpu.SemaphoreType.DMA((2,)), pltpu.SemaphoreType.DMA(())],
    )
    return pl.pallas_call(
        functools.partial(_scatter_kernel, n_experts=n_experts, n_slots=n_slots, tmx=tmx,
                          max_empty_tiles=n_sorted // tmx - (TOP_K * n) // tmx),
        grid_spec=grid_spec,
        out_shape=jax.ShapeDtypeStruct((n_sorted, d), F32),
        compiler_params=pltpu.CompilerParams(dimension_semantics=("arbitrary",),
                                             vmem_limit_bytes=VMEM_LIMIT, has_side_effects=True),
        name="moe_scatter",
    )(pc, seg, dst, gaps, meta, hn)


EXPERT_SUB_ROWS = 512


def _expert_ffn_kernel(te_ref, na_ref, x_ref, wg_ref, wu_ref, wd_ref, y_ref, xb_ref, acc_ref):
    del te_ref
    i = pl.program_id(0)
    j = pl.program_id(1)

    @pl.when(i < na_ref[0])
    def _():
        @pl.when(j == 0)
        def _():
            xb_ref[...] = x_ref[...].astype(BF16)
            acc_ref[...] = jnp.zeros_like(acc_ref)

        wg = wg_ref[...].astype(BF16)
        wu = wu_ref[...].astype(BF16)
        wd = wd_ref[...].astype(BF16)
        sub_rows = min(EXPERT_SUB_ROWS, xb_ref.shape[0])
        for r0 in range(0, xb_ref.shape[0], sub_rows):
            rows = slice(r0, r0 + sub_rows)
            xb = xb_ref[rows, :]
            act = (_silu(_dot(xb, wg)) * _dot(xb, wu)).astype(BF16)
            acc_ref[rows, :] += _dot(act, wd)

        @pl.when(j == pl.num_programs(1) - 1)
        def _():
            y_ref[...] = acc_ref[...]

    @pl.when((i >= na_ref[0]) & (j == pl.num_programs(1) - 1))
    def _():
        y_ref[...] = jnp.zeros_like(y_ref)


def _expert_ffn(xs, tile_expert, n_active, w_gate_up_bf16, w_down_bf16, tmx):
    n_sorted, d = xs.shape
    f = w_down_bf16.shape[1]
    fb = _hidden_block(f, TILES["fbx"])
    nf = f // fb
    n_tiles = n_sorted // tmx

    def row_map(i, j, te, na):
        return (jnp.minimum(i, na[0] - 1), 0)

    def col(i, j, na):
        return jnp.where(i < na[0], j, nf - 1)

    grid_spec = pltpu.PrefetchScalarGridSpec(
        num_scalar_prefetch=2,
        grid=(n_tiles, nf),
        in_specs=[
            pl.BlockSpec((tmx, d), row_map),
            pl.BlockSpec((None, d, fb), lambda i, j, te, na: (te[i], 0, col(i, j, na))),
            pl.BlockSpec((None, d, fb), lambda i, j, te, na: (te[i], 0, col(i, j, na) + nf)),
            pl.BlockSpec((None, fb, d), lambda i, j, te, na: (te[i], col(i, j, na), 0)),
        ],
        out_specs=pl.BlockSpec((tmx, d), lambda i, j, te, na: (i, 0)),
        scratch_shapes=[pltpu.VMEM((tmx, d), BF16), pltpu.VMEM((tmx, d), F32)],
    )
    return pl.pallas_call(
        _expert_ffn_kernel,
        grid_spec=grid_spec,
        out_shape=jax.ShapeDtypeStruct((n_sorted, d), F32),
        compiler_params=_cparams(("arbitrary", "arbitrary")),
        name="expert_ffn",
    )(tile_expert, n_active, xs, w_gate_up_bf16, w_gate_up_bf16, w_down_bf16)


def _combine_kernel(pc_ref, seg_ref, dst_ref, meta_ref, h_ref, ys_ref, o_ref, buf_ref, sem,
                    *, n_experts, n_slots):
    tile = pl.program_id(0)
    last = pl.num_programs(0) - 1
    max_rows = h_ref.shape[0]

    def copies(t, b, start):
        def make_copy(local, glob, size):
            return pltpu.make_async_copy(ys_ref.at[pl.ds(glob, size)], buf_ref.at[b, pl.ds(local, size)],
                                         sem.at[b])
        _segment_copies(pc_ref, seg_ref, dst_ref, t, n_experts, max_rows, make_copy, start)

    @pl.when(tile == 0)
    def _():
        buf_ref[...] = jnp.zeros_like(buf_ref)
        copies(0, 0, True)

    gates = _slot_one_hot(meta_ref, n_slots, True)
    for b in range(2):
        @pl.when(tile % 2 == b)
        def _(b=b):
            @pl.when(tile < last)
            def _():
                copies(tile + 1, 1 - b, True)

            copies(tile, b, False)
            o_ref[...] = h_ref[...] + _dot_tn(gates, buf_ref[b].astype(BF16))


def _combine(h, meta, pc, seg, dst, ys, tm, n_experts):
    n, d = h.shape
    n_slots = TOP_K * tm + ROW_ALIGN * n_experts
    grid_spec = pltpu.PrefetchScalarGridSpec(
        num_scalar_prefetch=3,
        grid=(n // tm,),
        in_specs=[pl.BlockSpec((8, tm), lambda i, *_: (0, i)),
                  pl.BlockSpec((tm, d), lambda i, *_: (i, 0)),
                  pl.BlockSpec(memory_space=pl.ANY)],
        out_specs=pl.BlockSpec((tm, d), lambda i, *_: (i, 0)),
        scratch_shapes=[pltpu.VMEM((2, n_slots, d), F32), pltpu.SemaphoreType.DMA((2,))],
    )
    return pl.pallas_call(
        functools.partial(_combine_kernel, n_experts=n_experts, n_slots=n_slots),
        grid_spec=grid_spec,
        out_shape=jax.ShapeDtypeStruct((n, d), F32),
        compiler_params=_cparams(("arbitrary",)),
        name="moe_combine",
    )(pc, seg, dst, meta, h, ys)


def _row(vec):
    return vec.astype(F32).reshape(1, -1)


def _round_up(x, m):
    return (x + m - 1) // m * m


def kernel(x, lower_bounds, l0_mix_norm, l0_hgrn_w_in, l0_hgrn_out_norm, l0_hgrn_w_out, l0_ffn_norm, l0_ffn_w_gate_up, l0_ffn_w_down, l1_mix_norm, l1_diff_w_in, l1_q_norm, l1_k_norm, l1_lambda_q1, l1_lambda_k1, l1_lambda_q2, l1_lambda_k2, l1_diff_sub_norm, l1_diff_w_out, l1_ffn_norm, l1_router, l1_moe_w_gate_up, l1_moe_w_down):
    batch, seq, d = x.shape
    n = batch * seq
    heads = d // LANE
    n_experts = l1_router.shape[1]
    x2 = x.reshape(n, d).astype(F32)

    lb0 = jnp.cumsum(jax.nn.softmax(lower_bounds.astype(F32), axis=0), axis=0)[0]
    q, k, lf, v, gs = _hgrn_in(x2, _row(l0_mix_norm), _row(lb0), l0_hgrn_w_in.astype(BF16))
    og = _hgrn_rec(q, k, lf, v, gs, _row(l0_hgrn_out_norm), batch, seq)
    h = _ffn(og, l0_hgrn_w_out.astype(BF16), x2, _row(l0_ffn_norm),
             l0_ffn_w_gate_up.astype(BF16), l0_ffn_w_down.astype(BF16))

    lam_init = 0.8 - 0.6 * math.exp(-0.3 * 1)
    q, k, vt = _diff_in(h, _row(l1_mix_norm), l1_diff_w_in.astype(BF16),
                        _row(jnp.tile(l1_q_norm, 2 * heads)), _row(jnp.tile(l1_k_norm, 2 * heads)), batch, seq)
    slopes = 2.0 ** (-8.0 * jnp.arange(1, heads + 1, dtype=F32) / heads)
    lam_rows = jnp.zeros((8, LANE), F32).at[:4, :DIFF_HEAD_DIM].set(
        jnp.stack([l1_lambda_q1, l1_lambda_k1, l1_lambda_q2, l1_lambda_k2]).astype(F32))
    oa = _attention(q, k, vt, slopes, lam_rows, _row(l1_diff_sub_norm), batch, seq, lam_init)

    h, hn, meta, tile_cnt = _proj_router(oa, l1_diff_w_out.astype(BF16), h, _row(l1_ffn_norm), l1_router)
    tm = min(TILES["tm_router"], n)
    n_tok_tiles = n // tm
    tmx = min(TILES["tmx"], n)
    cnt = tile_cnt.reshape(n_tok_tiles, ROUTE_ROWS, LANE)[:, :n_experts, 0].astype(I32)
    pc = _round_up(cnt, ROW_ALIGN)
    seg = jnp.cumsum(pc, axis=1) - pc
    region = _round_up(jnp.sum(pc, axis=0), tmx)
    region_end = jnp.cumsum(region)
    dst = (region_end - region)[None, :] + jnp.cumsum(pc, axis=0) - pc
    n_x_tiles = (TOP_K * n + n_tok_tiles * n_experts * (ROW_ALIGN - 1)) // tmx + n_experts
    n_active = (region_end[-1:] // tmx).astype(I32)
    x_tile_start = jnp.arange(n_x_tiles, dtype=I32) * tmx
    tile_expert = jnp.sum((x_tile_start[:, None] >= region_end[None, :]).astype(I32), axis=1)
    tile_expert = jnp.minimum(tile_expert, tile_expert[n_active[0] - 1]).astype(I32)
    pc_f, seg_f, dst_f = (t.reshape(-1).astype(I32) for t in (pc, seg, dst))
    used = jnp.sum(pc, axis=0)
    gaps = jnp.concatenate([jnp.stack([region_end - region + used, region - used], axis=1).reshape(-1),
                            region_end[-1:], n_x_tiles - n_active]).astype(I32)

    xs = _scatter_rows(hn, meta, pc_f, seg_f, dst_f, gaps, n_x_tiles * tmx, tm, tmx, n_experts)
    ys = _expert_ffn(xs, tile_expert, n_active, l1_moe_w_gate_up, l1_moe_w_down, tmx)
    out = _combine(h, meta, pc_f, seg_f, dst_f, ys, tm, n_experts)
    return out.reshape(batch, seq, d).astype(x.dtype)
```

```python
import functools
import math

import jax
import jax.numpy as jnp
from jax import lax
from jax.experimental import pallas as pl
from jax.experimental.pallas import tpu as pltpu

F32 = jnp.float32
BF16 = jnp.bfloat16
I32 = jnp.int32

NORM_EPS = 1e-6
LANE = 128
HGRN_CHUNK = 64
DIFF_HEAD_DIM = 64
TOP_K = 2
LOG2E = math.log2(math.e)
EXP_CLAMP = 60.0
VMEM_LIMIT = 56 * 1024 * 1024

TILES = dict(
    tm_in=512,
    tb_rec=512,
    tm_ffn=512,
    tq=1024,
    tm_router=512,
    tmx=1024,
    fbx=512,
)


def _cparams(sem):
    return pltpu.CompilerParams(dimension_semantics=sem, vmem_limit_bytes=VMEM_LIMIT)


def _sigmoid(x):
    return 1.0 / (1.0 + jnp.exp(-x))


def _silu(x):
    return x * _sigmoid(x)


def _rms_rows(x, gain):
    ms = jnp.mean(x * x, axis=-1, keepdims=True)
    return x * lax.rsqrt(ms + NORM_EPS) * gain


def _dot(a, b):
    return jnp.dot(a, b, preferred_element_type=F32)


def _dot_nt(a, b):
    return lax.dot_general(a, b, (((1,), (1,)), ((), ())), preferred_element_type=F32)


def _dot_tn(a, b):
    return lax.dot_general(a, b, (((0,), (0,)), ((), ())), preferred_element_type=F32)


def _hgrn_in_kernel(x_ref, g_ref, lb_ref, w_ref, q_ref, k_ref, lf_ref, v_ref, gs_ref):
    d = x_ref.shape[1]
    xn = _rms_rows(x_ref[...], g_ref[...]).astype(BF16)

    def proj(j):
        return _dot(xn, w_ref[:, j * d:(j + 1) * d])

    q_ref[...] = _silu(proj(0)).astype(BF16)
    lb = lb_ref[...]
    forget = lb + (1.0 - lb) * _sigmoid(proj(1))
    k_ref[...] = (1.0 - forget).astype(BF16)
    lf_ref[...] = jnp.log(forget)
    v_ref[...] = proj(2).astype(BF16)
    gs_ref[...] = _silu(proj(3)).astype(BF16)


def _hgrn_in(x, gain, lb, w_bf16):
    n, d = x.shape
    tm = min(TILES["tm_in"], n)
    row = pl.BlockSpec((tm, d), lambda i: (i, 0))
    vec = pl.BlockSpec((1, d), lambda i: (0, 0))
    return pl.pallas_call(
        _hgrn_in_kernel,
        grid=(n // tm,),
        in_specs=[row, vec, vec, pl.BlockSpec((d, 4 * d), lambda i: (0, 0))],
        out_specs=[row, row, row, row, row],
        out_shape=[
            jax.ShapeDtypeStruct((n, d), BF16),
            jax.ShapeDtypeStruct((n, d), BF16),
            jax.ShapeDtypeStruct((n, d), F32),
            jax.ShapeDtypeStruct((n, d), BF16),
            jax.ShapeDtypeStruct((n, d), BF16),
        ],
        compiler_params=_cparams(("arbitrary",)),
        name="hgrn_in",
    )(x, gain, lb, w_bf16)


def _hgrn_rec_kernel(q_ref, k_ref, lf_ref, v_ref, gs_ref, on_ref, o_ref, st_ref, *, chunk, heads):
    @pl.when(pl.program_id(1) == 0)
    def _():
        st_ref[...] = jnp.zeros_like(st_ref)

    c = chunk
    n_chunks = q_ref.shape[0] // c
    row = lax.broadcasted_iota(I32, (c, c), 0)
    col = lax.broadcasted_iota(I32, (c, c), 1)
    causal = col <= row
    tri = causal.astype(BF16)
    mid = c // 2 - 1

    def body(ci, carry):
        r0 = pl.multiple_of(ci * c, c)
        rows = pl.ds(r0, c)
        lf = lf_ref[rows, :]
        hi = lf.astype(BF16)
        lo = (lf - hi.astype(F32)).astype(BF16)
        b = _dot(tri, hi) + _dot(tri, lo)
        bmid = b[mid:mid + 1, :]
        bend = b[c - 1:c, :]
        e_q = jnp.exp(jnp.minimum(b - bmid, EXP_CLAMP))
        e_k = jnp.exp(jnp.minimum(bmid - b, EXP_CLAMP))
        qt = (q_ref[rows, :].astype(F32) * e_q).astype(BF16)
        kt = (k_ref[rows, :].astype(F32) * e_k).astype(BF16)
        v = v_ref[rows, :]
        gs = gs_ref[rows, :].astype(F32)
        e_mid = jnp.exp(bmid)
        e_end = jnp.exp(bend)
        e_end_mid = jnp.exp(bend - bmid)
        for h in range(heads):
            sl = slice(h * LANE, (h + 1) * LANE)
            qh, kh, vh = qt[:, sl], kt[:, sl], v[:, sl]
            a = jnp.where(causal, _dot_nt(qh, kh), 0.0).astype(BF16)
            st = st_ref[h]
            o = _dot(a, vh) + _dot_nt(qh, (st * e_mid[:, sl]).astype(BF16))
            st_ref[h] = st * e_end[:, sl] + _dot_tn(vh, kh) * e_end_mid[:, sl]
            og = _rms_rows(o, on_ref[:, sl]) * gs[:, sl]
            o_ref[rows, sl] = og.astype(BF16)
        return carry

    lax.fori_loop(0, n_chunks, body, 0, unroll=True)


def _hgrn_rec(q, k, lf, v, gs, out_norm, batch, seq):
    n, d = q.shape
    heads = d // LANE
    tb = min(TILES["tb_rec"], seq)
    nt = seq // tb
    blk = pl.BlockSpec((tb, d), lambda b, t: (b * nt + t, 0))
    return pl.pallas_call(
        functools.partial(_hgrn_rec_kernel, chunk=HGRN_CHUNK, heads=heads),
        grid=(batch, nt),
        in_specs=[blk, blk, blk, blk, blk, pl.BlockSpec((1, d), lambda b, t: (0, 0))],
        out_specs=blk,
        out_shape=jax.ShapeDtypeStruct((n, d), BF16),
        scratch_shapes=[pltpu.VMEM((heads, LANE, LANE), F32)],
        compiler_params=_cparams(("arbitrary", "arbitrary")),
        name="hgrn_rec",
    )(q, k, lf, v, gs, out_norm)


def _ffn_kernel(a_ref, wo_ref, x_ref, g_ref, wgu_ref, wd_ref, o_ref):
    f = wd_ref.shape[0]
    h = x_ref[...] + _dot(a_ref[...], wo_ref[...])
    xn = _rms_rows(h, g_ref[...]).astype(BF16)
    gu = _dot(xn, wgu_ref[...])
    act = (_silu(gu[:, :f]) * gu[:, f:]).astype(BF16)
    o_ref[...] = h + _dot(act, wd_ref[...])


def _hidden_block(f, want):
    best = None
    for fb in range(LANE, f + 1, LANE):
        if f % fb == 0 and fb <= want:
            best = fb
    assert best is not None, f
    return best


def _ffn(mix, w_out_bf16, x, gain, w_gate_up_bf16, w_down_bf16):
    n, d = x.shape
    f = w_down_bf16.shape[0]
    tm = min(TILES["tm_ffn"], n)
    row = pl.BlockSpec((tm, d), lambda i: (i, 0))
    once = pl.Buffered(1)
    return pl.pallas_call(
        _ffn_kernel,
        grid=(n // tm,),
        in_specs=[
            row,
            pl.BlockSpec((d, d), lambda i: (0, 0), pipeline_mode=once),
            row,
            pl.BlockSpec((1, d), lambda i: (0, 0)),
            pl.BlockSpec((d, 2 * f), lambda i: (0, 0), pipeline_mode=once),
            pl.BlockSpec((f, d), lambda i: (0, 0), pipeline_mode=once),
        ],
        out_specs=row,
        out_shape=jax.ShapeDtypeStruct((n, d), F32),
        compiler_params=_cparams(("arbitrary",)),
        name="ffn",
    )(mix, w_out_bf16, x, gain, w_gate_up_bf16, w_down_bf16)


def _group_rms(y, gmat, gain):
    d = y.shape[1]
    sq = (y * y).astype(BF16)
    ms = jnp.concatenate(
        [_dot(sq[:, s * LANE:(s + 1) * LANE], gmat) for s in range(d // LANE)], axis=1)
    return y * lax.rsqrt(ms + NORM_EPS) * gain


V_ROWS = LANE + 16


def _diff_in_kernel(x_ref, g_ref, w_ref, gm_ref, qn_ref, kn_ref, q_ref, k_ref, vt_ref):
    d = x_ref.shape[1]
    xn = _rms_rows(x_ref[...], g_ref[...]).astype(BF16)

    def proj(j):
        return _dot(xn, w_ref[:, j * d:(j + 1) * d])

    q_ref[...] = (_group_rms(proj(0), gm_ref[...], qn_ref[...]) * (DIFF_HEAD_DIM ** -0.5 * LOG2E)).astype(BF16)
    k_ref[...] = _group_rms(proj(1), gm_ref[...], kn_ref[...]).astype(BF16)
    v = proj(2)
    for h in range(d // LANE):
        vt_ref[h, :LANE, :] = v[:, h * LANE:(h + 1) * LANE].T.astype(BF16)
        vt_ref[h, LANE:, :] = jnp.ones((V_ROWS - LANE, v.shape[0]), BF16)


def _diff_in(x, gain, w_bf16, q_norm_row, k_norm_row, batch, seq):
    n, d = x.shape
    heads = d // LANE
    tm = min(TILES["tm_in"], seq)
    nt = seq // tm
    lane_group = jnp.arange(LANE) // DIFF_HEAD_DIM
    gmat = ((lane_group[:, None] == lane_group[None, :]).astype(F32) / DIFF_HEAD_DIM).astype(BF16)
    row = pl.BlockSpec((tm, d), lambda i: (i, 0))
    vec = pl.BlockSpec((1, d), lambda i: (0, 0))
    return pl.pallas_call(
        _diff_in_kernel,
        grid=(n // tm,),
        in_specs=[row, vec, pl.BlockSpec((d, 3 * d), lambda i: (0, 0)),
                  pl.BlockSpec((LANE, LANE), lambda i: (0, 0)), vec, vec],
        out_specs=[row, row, pl.BlockSpec((None, heads, V_ROWS, tm), lambda i: (i // nt, 0, 0, i % nt))],
        out_shape=[jax.ShapeDtypeStruct((n, d), BF16), jax.ShapeDtypeStruct((n, d), BF16),
                   jax.ShapeDtypeStruct((batch, heads, V_ROWS, seq), BF16)],
        compiler_params=_cparams(("arbitrary",)),
        name="diff_in",
    )(x, gain, w_bf16, gmat, q_norm_row, k_norm_row)


ATTN_UNROLL = 4
FULL, DIAG_A, DIAG_B = range(3)


def _attn_kernel(slope_ref, q_ref, k_ref, vt_ref, lam_ref, sn_ref, o_ref, m_ref, acc_ref, w_ref,
                 *, tq, lam_init):
    h = pl.program_id(1)
    qb = pl.program_id(2)
    slope = slope_ref[h] * LOG2E

    q = q_ref[...]
    lane = lax.broadcasted_iota(I32, q.shape, 1)
    zero = jnp.zeros_like(q)
    qs = jnp.concatenate([jnp.where(lane < DIFF_HEAD_DIM, q, zero),
                          jnp.where(lane >= DIFF_HEAD_DIM, q, zero)], axis=0)

    nq2 = 2 * tq
    m_ref[...] = jnp.full_like(m_ref, -jnp.inf)
    acc_ref[...] = jnp.zeros_like(acc_ref)

    lane_k = lax.broadcasted_iota(I32, (tq, LANE), 1)
    bias = slope * lax.broadcasted_iota(I32, (tq, LANE), 0).astype(F32)
    bias_hi = bias.astype(BF16).astype(F32)
    bias_cols = jnp.where(lane_k == 0, bias_hi, jnp.where(lane_k == 1, bias - bias_hi, 0.0)).astype(BF16)
    lane_q = lax.broadcasted_iota(I32, (2 * tq, LANE), 1)
    qsx = jnp.concatenate([qs, jnp.where(lane_q < 2, 1.0, 0.0).astype(BF16)], axis=1)

    half = tq // 2
    qsx_b = jnp.concatenate([qsx[half:tq], qsx[tq + half:]], axis=0)

    def put(buf, kind, kb):
        k0 = pl.multiple_of(kb * tq, tq)
        if kind == FULL:
            w_ref[buf, :, :nq2] = _dot_nt(jnp.concatenate([k_ref[pl.ds(k0, tq), :], bias_cols], axis=1), qsx)
        elif kind == DIAG_A:
            kx = jnp.concatenate([k_ref[pl.ds(k0, half), :], bias_cols[:half]], axis=1)
            w_ref[buf, :half, :nq2] = _dot_nt(kx, qsx)
        else:
            kx = jnp.concatenate([k_ref[pl.ds(k0 + half, half), :], bias_cols[half:]], axis=1)
            w_ref[buf, :half, :tq] = _dot_nt(kx, qsx_b)

    def consume(buf, kind, kb):
        k0 = pl.multiple_of(kb * tq, tq)
        if kind == FULL:
            w = w_ref[buf, :, :nq2]
            keys = pl.ds(k0, tq)
            off = slope * ((kb - qb) * tq).astype(F32)
        else:
            period = tq if kind == DIAG_A else half
            w = w_ref[buf, :half, :nq2] if kind == DIAG_A else w_ref[buf, :half, :tq]
            keys = pl.ds(k0, half) if kind == DIAG_A else pl.ds(k0 + half, half)
            r = lax.broadcasted_iota(I32, w.shape, 0)
            c = lax.broadcasted_iota(I32, w.shape, 1)
            w = jnp.where(r <= jnp.where(c >= period, c - period, c), w, -jnp.inf)
            off = 0.0
        if kind == DIAG_B:
            m_old = jnp.concatenate([m_ref[:, half:tq], m_ref[:, tq + half:]], axis=1)
        else:
            m_old = m_ref[...]
        m_new = jnp.maximum(m_old, jnp.max(w, axis=0, keepdims=True) + off)
        alpha = jnp.exp2(m_old - m_new)
        p = jnp.exp2(w - (m_new - off)).astype(BF16)
        pv = _dot(vt_ref[:, keys], p)
        if kind == DIAG_B:
            acc_ref[:, half:tq] = alpha[:, :half] * acc_ref[:, half:tq] + pv[:, :half]
            acc_ref[:, tq + half:nq2] = alpha[:, half:] * acc_ref[:, tq + half:nq2] + pv[:, half:]
            m_ref[:, half:tq] = m_new[:, :half]
            m_ref[:, tq + half:] = m_new[:, half:]
        else:
            acc_ref[:, :nq2] = alpha * acc_ref[:, :nq2] + pv
            m_ref[...] = m_new

    put(0, DIAG_B, qb)

    @pl.when(qb == 0)
    def _():
        put(1, DIAG_A, qb)
        consume(0, DIAG_B, qb)
        consume(1, DIAG_A, qb)

    @pl.when(qb > 0)
    def _():
        put(1, FULL, 0)
        consume(0, DIAG_B, qb)

    def group(g, carry):
        first = ATTN_UNROLL * g
        for u in range(ATTN_UNROLL):
            put(u % 2, FULL, first + u + 1)
            consume((u + 1) % 2, FULL, first + u)
        return carry

    n_groups = jnp.maximum(qb - 1, 0) // ATTN_UNROLL
    lax.fori_loop(0, n_groups, group, 0)
    rest = qb - ATTN_UNROLL * n_groups
    for r in range(1, ATTN_UNROLL + 1):
        @pl.when(rest == r)
        def _(r=r):
            for u in range(r):
                if u + 1 < r:
                    put(u % 2, FULL, qb - r + u + 1)
                else:
                    put(u % 2, DIAG_A, qb)
                consume((u + 1) % 2, FULL, qb - r + u)
            consume((r + 1) % 2, DIAG_A, qb)

    lam_rows = lam_ref[...]
    lam = (jnp.exp(jnp.sum(lam_rows[0:1] * lam_rows[1:2], axis=1, keepdims=True))
           - jnp.exp(jnp.sum(lam_rows[2:3] * lam_rows[3:4], axis=1, keepdims=True)) + lam_init)
    acc = acc_ref[:, :nq2]
    on = acc[:LANE] / acc[LANE:LANE + 1]
    o = (on[:, :tq] - lam * on[:, tq:]).T
    o_ref[...] = (_rms_rows(o, sn_ref[...]) * (1.0 - lam_init)).astype(BF16)


def _attention(q, k, vt, slopes, lam_rows, sub_norm_row, batch, seq, lam_init):
    n, d = q.shape
    heads = d // LANE
    tq = min(TILES["tq"], seq)
    nq = seq // tq
    grid_spec = pltpu.PrefetchScalarGridSpec(
        num_scalar_prefetch=1,
        grid=(batch, heads, nq),
        in_specs=[
            pl.BlockSpec((tq, LANE), lambda b, h, i, s: (b * nq + i, h)),
            pl.BlockSpec((seq, LANE), lambda b, h, i, s: (b, h)),
            pl.BlockSpec((None, None, V_ROWS, seq), lambda b, h, i, s: (b, h, 0, 0)),
            pl.BlockSpec((8, LANE), lambda b, h, i, s: (0, 0)),
            pl.BlockSpec((1, LANE), lambda b, h, i, s: (0, 0)),
        ],
        out_specs=pl.BlockSpec((tq, LANE), lambda b, h, i, s: (b * nq + i, h)),
        scratch_shapes=[pltpu.VMEM((1, 2 * tq), F32), pltpu.VMEM((V_ROWS, 2 * tq + LANE), F32),
                        pltpu.VMEM((2, tq, 2 * tq + LANE), F32)],
    )
    return pl.pallas_call(
        functools.partial(_attn_kernel, tq=tq, lam_init=lam_init),
        grid_spec=grid_spec,
        out_shape=jax.ShapeDtypeStruct((n, d), BF16),
        compiler_params=_cparams(("arbitrary", "arbitrary", "arbitrary")),
        name="diff_attn",
    )(slopes, q, k, vt, lam_rows, sub_norm_row)


ROUTE_ROWS = 16
ROW_ALIGN = 8


def _proj_router_kernel(a_ref, w_ref, r_ref, g_ref, rw_ref, triu_ref, h_ref, hn_ref, meta_ref, cnt_ref,
                        *, n_experts):
    h = r_ref[...] + _dot(a_ref[...], w_ref[...])
    h_ref[...] = h
    hn = _rms_rows(h, g_ref[...])
    hn_hi = hn.astype(BF16)
    hn_ref[...] = hn_hi
    hn_lo = (hn - hn_hi.astype(F32)).astype(BF16)
    both = _dot(hn_hi, rw_ref[...])
    logits = both[:, :LANE] + both[:, LANE:] + _dot(hn_lo, rw_ref[:, :LANE])
    lt = logits.T[:ROUTE_ROWS]
    sub = lax.broadcasted_iota(I32, lt.shape, 0)
    subf = sub.astype(F32)
    neg = jnp.float32(-jnp.inf)
    lt = jnp.where(sub < n_experts, lt, neg)
    v1 = jnp.max(lt, axis=0, keepdims=True)
    i1 = jnp.min(jnp.where(lt == v1, subf, float(ROUTE_ROWS)), axis=0, keepdims=True)
    oh1 = subf == i1
    rest = jnp.where(oh1, neg, lt)
    v2 = jnp.max(rest, axis=0, keepdims=True)
    i2 = jnp.min(jnp.where(rest == v2, subf, float(ROUTE_ROWS)), axis=0, keepdims=True)
    oh2 = subf == i2
    e = jnp.exp(v2 - v1)
    w1 = 1.0 / (1.0 + e)
    w2 = e * w1
    cnt = jnp.where(oh1, 1.0, 0.0) + jnp.where(oh2, 1.0, 0.0)
    before = _dot(cnt.astype(BF16), triu_ref[...])
    counts = jnp.sum(cnt, axis=1, keepdims=True) + jnp.zeros((ROUTE_ROWS, LANE), F32)
    padded = jnp.floor((counts + (ROW_ALIGN - 1)) * (1.0 / ROW_ALIGN)) * ROW_ALIGN
    er = lax.broadcasted_iota(I32, (ROUTE_ROWS, ROUTE_ROWS), 0)
    ec = lax.broadcasted_iota(I32, (ROUTE_ROWS, ROUTE_ROWS), 1)
    seg = _dot(jnp.where(ec < er, 1.0, 0.0).astype(BF16), padded.astype(BF16))[:, 0:1]
    s1 = jnp.sum(jnp.where(oh1, before + seg, 0.0), axis=0, keepdims=True)
    s2 = jnp.sum(jnp.where(oh2, before + seg, 0.0), axis=0, keepdims=True)
    meta = jnp.where(sub == 0, i1, jnp.where(sub == 1, i2, jnp.where(sub == 2, w1, jnp.where(
        sub == 3, w2, jnp.where(sub == 4, s1, jnp.where(sub == 5, s2, 0.0))))))
    meta_ref[...] = meta[:8]
    cnt_ref[...] = counts


def _proj_router(a, w_bf16, res, gain, router):
    n, d = res.shape
    n_experts = router.shape[1]
    assert n_experts <= ROUTE_ROWS
    tm = min(TILES["tm_router"], n)
    rw = jnp.zeros((d, LANE), F32).at[:, :n_experts].set(router.astype(F32))
    rw_hi = rw.astype(BF16)
    rw_cat = jnp.concatenate([rw_hi, (rw - rw_hi.astype(F32)).astype(BF16)], axis=1)
    triu = jnp.triu(jnp.ones((tm, tm), F32), 1).astype(BF16)
    row = pl.BlockSpec((tm, d), lambda i: (i, 0))
    return pl.pallas_call(
        functools.partial(_proj_router_kernel, n_experts=n_experts),
        grid=(n // tm,),
        in_specs=[row, pl.BlockSpec((d, d), lambda i: (0, 0)), row,
                  pl.BlockSpec((1, d), lambda i: (0, 0)),
                  pl.BlockSpec((d, 2 * LANE), lambda i: (0, 0)),
                  pl.BlockSpec((tm, tm), lambda i: (0, 0))],
        out_specs=[row, row, pl.BlockSpec((8, tm), lambda i: (0, i)),
                   pl.BlockSpec((ROUTE_ROWS, LANE), lambda i: (i, 0))],
        out_shape=[jax.ShapeDtypeStruct((n, d), F32), jax.ShapeDtypeStruct((n, d), BF16),
                   jax.ShapeDtypeStruct((8, n), F32),
                   jax.ShapeDtypeStruct((n // tm * ROUTE_ROWS, LANE), F32)],
        compiler_params=_cparams(("arbitrary",)),
        name="proj_router",
    )(a, w_bf16, res, gain, rw_cat, triu)


def _for_each_piece(n_rows, max_rows, fn):
    size = ROW_ALIGN
    while size * 2 <= max_rows:
        size *= 2
    done = 0
    while size >= ROW_ALIGN:
        take = (n_rows & size) != 0

        @pl.when(take)
        def _(done=done, size=size):
            fn(done, size)

        done = done + jnp.where(take, size, 0)
        size //= 2


def _segment_copies(pc_ref, seg_ref, dst_ref, tile, n_experts, max_rows, make_copy, start):
    for e in range(n_experts):
        idx = tile * n_experts + e
        local0 = seg_ref[idx]
        sorted0 = dst_ref[idx]

        def piece(off, size, local0=local0, sorted0=sorted0, e=e):
            cp = make_copy(pl.multiple_of(local0 + off, ROW_ALIGN), pl.multiple_of(sorted0 + off, ROW_ALIGN), size)
            if start:
                cp.start(priority=e % 2)
            else:
                cp.wait()

        _for_each_piece(pc_ref[idx], max_rows, piece)


def _slot_one_hot(meta_ref, n_slots, gated):
    tokens = meta_ref.shape[1]
    sub = lax.broadcasted_iota(I32, (n_slots, tokens), 0)
    out = None
    for kk in range(TOP_K):
        slot = meta_ref[4 + kk:5 + kk, :].astype(I32)
        val = meta_ref[2 + kk:3 + kk, :] if gated else 1.0
        term = jnp.where(sub == slot, val, 0.0)
        out = term if out is None else out + term
    return out.astype(BF16)


ZERO_ROWS = 512


def _scatter_kernel(pc_ref, seg_ref, dst_ref, gap_ref, meta_ref, hn_ref, xs_ref, sorted_ref, zeros_ref, sem, zsem,
                    *, n_experts, n_slots, tmx, max_empty_tiles):
    tile = pl.program_id(0)
    last = pl.num_programs(0) - 1
    max_rows = hn_ref.shape[0]

    def copies(t, b, start):
        def make_copy(local, glob, size):
            return pltpu.make_async_copy(sorted_ref.at[b, pl.ds(local, size)], xs_ref.at[pl.ds(glob, size)],
                                         sem.at[b])
        _segment_copies(pc_ref, seg_ref, dst_ref, t, n_experts, max_rows, make_copy, start)

    def zero_fill(start):
        def go(row0, size):
            cp = pltpu.make_async_copy(zeros_ref.at[pl.ds(0, size)],
                                       xs_ref.at[pl.ds(pl.multiple_of(row0, ROW_ALIGN), size)], zsem)
            if start:
                cp.start()
            else:
                cp.wait()

        zrows = zeros_ref.shape[0]
        for e in range(n_experts):
            row0 = gap_ref[2 * e]
            _for_each_piece(gap_ref[2 * e + 1], zrows, lambda off, size, row0=row0: go(row0 + off, size))
        for t in range(max_empty_tiles):
            @pl.when(t < gap_ref[2 * n_experts + 1])
            def _(t=t):
                for r0 in range(0, tmx, zrows):
                    go(gap_ref[2 * n_experts] + t * tmx + r0, zrows)

    @pl.when(tile == 0)
    def _():
        zeros_ref[...] = jnp.zeros_like(zeros_ref)
        zero_fill(True)
        zero_fill(False)

    one_hot = _slot_one_hot(meta_ref, n_slots, False)
    for b in range(2):
        @pl.when(tile % 2 == b)
        def _(b=b):
            sorted_ref[b] = _dot(one_hot, hn_ref[...])
            copies(tile, b, True)

            @pl.when(tile > 0)
            def _():
                copies(tile - 1, 1 - b, False)

            @pl.when(tile == last)
            def _():
                copies(tile, b, False)


def _scatter_rows(hn, meta, pc, seg, dst, gaps, n_sorted, tm, tmx, n_experts):
    n, d = hn.shape
    n_slots = TOP_K * tm + ROW_ALIGN * n_experts
    assert tmx % ZERO_ROWS == 0 or tmx < ZERO_ROWS
    grid_spec = pltpu.PrefetchScalarGridSpec(
        num_scalar_prefetch=4,
        grid=(n // tm,),
        in_specs=[pl.BlockSpec((8, tm), lambda i, *_: (0, i)),
                  pl.BlockSpec((tm, d), lambda i, *_: (i, 0))],
        out_specs=pl.BlockSpec(memory_space=pl.ANY),
        scratch_shapes=[pltpu.VMEM((2, n_slots, d), F32), pltpu.VMEM((min(ZERO_ROWS, tmx), d), F32),
                        pltpu.SemaphoreType.DMA((2,)), pltpu.SemaphoreType.DMA(())],
    )
    return pl.pallas_call(
        functools.partial(_scatter_kernel, n_experts=n_experts, n_slots=n_slots, tmx=tmx,
                          max_empty_tiles=n_sorted // tmx - (TOP_K * n) // tmx),
        grid_spec=grid_spec,
        out_shape=jax.ShapeDtypeStruct((n_sorted, d), F32),
        compiler_params=pltpu.CompilerParams(dimension_semantics=("arbitrary",),
                                             vmem_limit_bytes=VMEM_LIMIT, has_side_effects=True),
        name="moe_scatter",
    )(pc, seg, dst, gaps, meta, hn)


EXPERT_SUB_ROWS = 512


def _expert_ffn_kernel(te_ref, na_ref, x_ref, wg_ref, wu_ref, wd_ref, y_ref, xb_ref, acc_ref):
    del te_ref
    i = pl.program_id(0)
    j = pl.program_id(1)

    @pl.when(i < na_ref[0])
    def _():
        @pl.when(j == 0)
        def _():
            xb_ref[...] = x_ref[...].astype(BF16)
            acc_ref[...] = jnp.zeros_like(acc_ref)

        wg = wg_ref[...].astype(BF16)
        wu = wu_ref[...].astype(BF16)
        wd = wd_ref[...].astype(BF16)
        sub_rows = min(EXPERT_SUB_ROWS, xb_ref.shape[0])
        for r0 in range(0, xb_ref.shape[0], sub_rows):
            rows = slice(r0, r0 + sub_rows)
            xb = xb_ref[rows, :]
            act = (_silu(_dot(xb, wg)) * _dot(xb, wu)).astype(BF16)
            acc_ref[rows, :] += _dot(act, wd)

        @pl.when(j == pl.num_programs(1) - 1)
        def _():
            y_ref[...] = acc_ref[...]

    @pl.when((i >= na_ref[0]) & (j == pl.num_programs(1) - 1))
    def _():
        y_ref[...] = jnp.zeros_like(y_ref)


def _expert_ffn(xs, tile_expert, n_active, w_gate_up_bf16, w_down_bf16, tmx):
    n_sorted, d = xs.shape
    f = w_down_bf16.shape[1]
    fb = _hidden_block(f, TILES["fbx"])
    nf = f // fb
    n_tiles = n_sorted // tmx

    def row_map(i, j, te, na):
        return (jnp.minimum(i, na[0] - 1), 0)

    def col(i, j, na):
        return jnp.where(i < na[0], j, nf - 1)

    grid_spec = pltpu.PrefetchScalarGridSpec(
        num_scalar_prefetch=2,
        grid=(n_tiles, nf),
        in_specs=[
            pl.BlockSpec((tmx, d), row_map),
            pl.BlockSpec((None, d, fb), lambda i, j, te, na: (te[i], 0, col(i, j, na))),
            pl.BlockSpec((None, d, fb), lambda i, j, te, na: (te[i], 0, col(i, j, na) + nf)),
            pl.BlockSpec((None, fb, d), lambda i, j, te, na: (te[i], col(i, j, na), 0)),
        ],
        out_specs=pl.BlockSpec((tmx, d), lambda i, j, te, na: (i, 0)),
        scratch_shapes=[pltpu.VMEM((tmx, d), BF16), pltpu.VMEM((tmx, d), F32)],
    )
    return pl.pallas_call(
        _expert_ffn_kernel,
        grid_spec=grid_spec,
        out_shape=jax.ShapeDtypeStruct((n_sorted, d), F32),
        compiler_params=_cparams(("arbitrary", "arbitrary")),
        name="expert_ffn",
    )(tile_expert, n_active, xs, w_gate_up_bf16, w_gate_up_bf16, w_down_bf16)


def _combine_kernel(pc_ref, seg_ref, dst_ref, meta_ref, h_ref, ys_ref, o_ref, buf_ref, sem,
                    *, n_experts, n_slots):
    tile = pl.program_id(0)
    last = pl.num_programs(0) - 1
    max_rows = h_ref.shape[0]

    def copies(t, b, start):
        def make_copy(local, glob, size):
            return pltpu.make_async_copy(ys_ref.at[pl.ds(glob, size)], buf_ref.at[b, pl.ds(local, size)],
                                         sem.at[b])
        _segment_copies(pc_ref, seg_ref, dst_ref, t, n_experts, max_rows, make_copy, start)

    @pl.when(tile == 0)
    def _():
        buf_ref[...] = jnp.zeros_like(buf_ref)
        copies(0, 0, True)

    gates = _slot_one_hot(meta_ref, n_slots, True)
    for b in range(2):
        @pl.when(tile % 2 == b)
        def _(b=b):
            @pl.when(tile < last)
            def _():
                copies(tile + 1, 1 - b, True)

            copies(tile, b, False)
            o_ref[...] = h_ref[...] + _dot_tn(gates, buf_ref[b].astype(BF16))


def _combine(h, meta, pc, seg, dst, ys, tm, n_experts):
    n, d = h.shape
    n_slots = TOP_K * tm + ROW_ALIGN * n_experts
    grid_spec = pltpu.PrefetchScalarGridSpec(
        num_scalar_prefetch=3,
        grid=(n // tm,),
        in_specs=[pl.BlockSpec((8, tm), lambda i, *_: (0, i)),
                  pl.BlockSpec((tm, d), lambda i, *_: (i, 0)),
                  pl.BlockSpec(memory_space=pl.ANY)],
        out_specs=pl.BlockSpec((tm, d), lambda i, *_: (i, 0)),
        scratch_shapes=[pltpu.VMEM((2, n_slots, d), F32), pltpu.SemaphoreType.DMA((2,))],
    )
    return pl.pallas_call(
        functools.partial(_combine_kernel, n_experts=n_experts, n_slots=n_slots),
        grid_spec=grid_spec,
        out_shape=jax.ShapeDtypeStruct((n, d), F32),
        compiler_params=_cparams(("arbitrary",)),
        name="moe_combine",
    )(pc, seg, dst, meta, h, ys)


def _row(vec):
    return vec.astype(F32).reshape(1, -1)


def _round_up(x, m):
    return (x + m - 1) // m * m


def kernel(x, lower_bounds, l0_mix_norm, l0_hgrn_w_in, l0_hgrn_out_norm, l0_hgrn_w_out, l0_ffn_norm, l0_ffn_w_gate_up, l0_ffn_w_down, l1_mix_norm, l1_diff_w_in, l1_q_norm, l1_k_norm, l1_lambda_q1, l1_lambda_k1, l1_lambda_q2, l1_lambda_k2, l1_diff_sub_norm, l1_diff_w_out, l1_ffn_norm, l1_router, l1_moe_w_gate_up, l1_moe_w_down):
    batch, seq, d = x.shape
    n = batch * seq
    heads = d // LANE
    n_experts = l1_router.shape[1]
    x2 = x.reshape(n, d).astype(F32)

    lb0 = jnp.cumsum(jax.nn.softmax(lower_bounds.astype(F32), axis=0), axis=0)[0]
    q, k, lf, v, gs = _hgrn_in(x2, _row(l0_mix_norm), _row(lb0), l0_hgrn_w_in.astype(BF16))
    og = _hgrn_rec(q, k, lf, v, gs, _row(l0_hgrn_out_norm), batch, seq)
    h = _ffn(og, l0_hgrn_w_out.astype(BF16), x2, _row(l0_ffn_norm),
             l0_ffn_w_gate_up.astype(BF16), l0_ffn_w_down.astype(BF16))

    lam_init = 0.8 - 0.6 * math.exp(-0.3 * 1)
    q, k, vt = _diff_in(h, _row(l1_mix_norm), l1_diff_w_in.astype(BF16),
                        _row(jnp.tile(l1_q_norm, 2 * heads)), _row(jnp.tile(l1_k_norm, 2 * heads)), batch, seq)
    slopes = 2.0 ** (-8.0 * jnp.arange(1, heads + 1, dtype=F32) / heads)
    lam_rows = jnp.zeros((8, LANE), F32).at[:4, :DIFF_HEAD_DIM].set(
        jnp.stack([l1_lambda_q1, l1_lambda_k1, l1_lambda_q2, l1_lambda_k2]).astype(F32))
    oa = _attention(q, k, vt, slopes, lam_rows, _row(l1_diff_sub_norm), batch, seq, lam_init)

    h, hn, meta, tile_cnt = _proj_router(oa, l1_diff_w_out.astype(BF16), h, _row(l1_ffn_norm), l1_router)
    tm = min(TILES["tm_router"], n)
    n_tok_tiles = n // tm
    tmx = min(TILES["tmx"], n)
    cnt = tile_cnt.reshape(n_tok_tiles, ROUTE_ROWS, LANE)[:, :n_experts, 0].astype(I32)
    pc = _round_up(cnt, ROW_ALIGN)
    seg = jnp.cumsum(pc, axis=1) - pc
    region = _round_up(jnp.sum(pc, axis=0), tmx)
    region_end = jnp.cumsum(region)
    dst = (region_end - region)[None, :] + jnp.cumsum(pc, axis=0) - pc
    n_x_tiles = (TOP_K * n + n_tok_tiles * n_experts * (ROW_ALIGN - 1)) // tmx + n_experts
    n_active = (region_end[-1:] // tmx).astype(I32)
    x_tile_start = jnp.arange(n_x_tiles, dtype=I32) * tmx
    tile_expert = jnp.sum((x_tile_start[:, None] >= region_end[None, :]).astype(I32), axis=1)
    tile_expert = jnp.minimum(tile_expert, tile_expert[n_active[0] - 1]).astype(I32)
    pc_f, seg_f, dst_f = (t.reshape(-1).astype(I32) for t in (pc, seg, dst))
    used = jnp.sum(pc, axis=0)
    gaps = jnp.concatenate([jnp.stack([region_end - region + used, region - used], axis=1).reshape(-1),
                            region_end[-1:], n_x_tiles - n_active]).astype(I32)

    xs = _scatter_rows(hn, meta, pc_f, seg_f, dst_f, gaps, n_x_tiles * tmx, tm, tmx, n_experts)
    ys = _expert_ffn(xs, tile_expert, n_active, l1_moe_w_gate_up, l1_moe_w_down, tmx)
    out = _combine(h, meta, pc_f, seg_f, dst_f, ys, tm, n_experts)
    return out.reshape(batch, seq, d).astype(x.dtype)
```
